```python
import jax, jax.numpy as jnp
from jax import lax
import numpy as np

D_MODEL = 1024
BATCH = 4
SEQ = 4096
DEPTH = 2
DEC_BATCH = 32
DEC_SEQ = 1
PAST_LEN = 16384
PAGE_SIZE = 128

GMLP_WIDTH = D_MODEL
GMLP_GROUPS = 8
GMLP_GROUP_DIM = GMLP_WIDTH // GMLP_GROUPS
CHUNK = 128
N_HEADS = 8
HEAD_DIM = 128
ATTN_WIDTH = N_HEADS * HEAD_DIM
MOBA_BLOCK = 256
MOBA_TOPK = 3
Q_BLOCK = 32
D_FF = 2816
CONV_W = 3
EPS = 1e-6
IN_WIDTH = 2 * GMLP_WIDTH + 3 * ATTN_WIDTH + 2 * D_MODEL

kernel_name = 'moba_gmlp_gated_hybrid_step'


def rmsnorm(x, w):
    xf = x.astype(jnp.float32)
    xf = xf * lax.rsqrt(jnp.mean(xf * xf, axis=-1, keepdims=True) + EPS)
    return (xf * w.astype(jnp.float32)).astype(x.dtype)


def mixer_inputs(h, w_in, vnorm_w, qn_w, kn_w):
    n, t, _ = h.shape
    sizes = [GMLP_WIDTH, GMLP_WIDTH, ATTN_WIDTH, ATTN_WIDTH, ATTN_WIDTH, D_MODEL]
    cuts = [int(c) for c in np.cumsum(sizes)]
    u, v, q, k, vv, ga, gb = jnp.split(h @ w_in, cuts, axis=-1)
    heads = lambda z: z.reshape(n, t, N_HEADS, HEAD_DIM)
    return (jax.nn.gelu(u), rmsnorm(jax.nn.gelu(v), vnorm_w),
            rmsnorm(heads(q), qn_w), rmsnorm(heads(k), kn_w), heads(vv), ga, gb)


def causal_spatial(ws):
    return ws * jnp.tril(jnp.ones((CHUNK, CHUNK), ws.dtype))


def gmlp_prompt(u, v, ws, bs):
    n, s, _ = v.shape
    vc = v.reshape(n, s // CHUNK, CHUNK, GMLP_GROUPS, GMLP_GROUP_DIM)
    mixed = jnp.einsum('gij,ncjgd->ncigd', causal_spatial(ws), vc) + bs.T[None, None, :, :, None]
    return u * mixed.reshape(n, s, GMLP_WIDTH)


def gmlp_sample(u, v, ws, bs):
    n, t, _ = v.shape
    vg = v.reshape(n, t, GMLP_GROUPS, GMLP_GROUP_DIM)
    w = causal_spatial(ws)[:, :t, :t]
    mixed = jnp.einsum('gij,njgd->nigd', w, vg) + bs[:, :t].T[None, :, :, None]
    return u * mixed.reshape(n, t, GMLP_WIDTH)


def moba_prompt(q, k, v):
    n, s = q.shape[0], q.shape[1]
    nb = -(-s // MOBA_BLOCK)
    pad = nb * MOBA_BLOCK - s
    topk = min(MOBA_TOPK, nb)
    qh = q.transpose(0, 2, 1, 3)
    kh = jnp.pad(k.transpose(0, 2, 1, 3), ((0, 0), (0, 0), (0, pad), (0, 0)))
    vh = jnp.pad(v.transpose(0, 2, 1, 3), ((0, 0), (0, 0), (0, pad), (0, 0)))
    kb = kh.reshape(n, N_HEADS, nb, MOBA_BLOCK, HEAD_DIM)
    vb = vh.reshape(n, N_HEADS, nb, MOBA_BLOCK, HEAD_DIM)
    kbar = jnp.mean(kb.astype(jnp.float32), axis=3)
    qblk = jnp.arange(s) // MOBA_BLOCK
    gate = jnp.einsum('bhsd,bhnd->bhsn', qh.astype(jnp.float32), kbar)
    fully_past = jnp.arange(nb)[None, :] < qblk[:, None]
    gate = jnp.where(fully_past, gate, -jnp.inf)
    _, idx = lax.top_k(gate, topk)
    nq = s // Q_BLOCK
    q_blocks = qh.reshape(n, N_HEADS, nq, Q_BLOCK, HEAD_DIM).transpose(2, 0, 1, 3, 4)
    idx_blocks = idx.reshape(n, N_HEADS, nq, Q_BLOCK, topk).transpose(2, 0, 1, 3, 4)
    b_i = jnp.arange(n)[:, None, None, None]
    h_i = jnp.arange(N_HEADS)[None, :, None, None]
    scale = HEAD_DIM ** -0.5

    def attend_block(args):
        q_c, idx_c, c = args
        qpos = c * Q_BLOCK + jnp.arange(Q_BLOCK)
        ob = (c * Q_BLOCK) // MOBA_BLOCK
        k_sel = kb[b_i, h_i, idx_c]
        v_sel = vb[b_i, h_i, idx_c]
        k_own = lax.dynamic_index_in_dim(kb, ob, axis=2, keepdims=False)
        v_own = lax.dynamic_index_in_dim(vb, ob, axis=2, keepdims=False)
        valid_sel = jnp.broadcast_to((idx_c < ob)[..., None], idx_c.shape + (MOBA_BLOCK,))
        valid_sel = valid_sel.reshape(n, N_HEADS, Q_BLOCK, topk * MOBA_BLOCK)
        own_pos = ob * MOBA_BLOCK + jnp.arange(MOBA_BLOCK)
        valid_own = jnp.broadcast_to(own_pos[None, :] <= qpos[:, None], (n, N_HEADS, Q_BLOCK, MOBA_BLOCK))
        l_sel = jnp.einsum('bhqd,bhqnjd->bhqnj', q_c, k_sel, preferred_element_type=jnp.float32)
        l_sel = l_sel.reshape(n, N_HEADS, Q_BLOCK, topk * MOBA_BLOCK)
        l_own = jnp.einsum('bhqd,bhjd->bhqj', q_c, k_own, preferred_element_type=jnp.float32)
        logits = jnp.concatenate([l_sel, l_own], axis=-1) * scale
        mask = jnp.concatenate([valid_sel, valid_own], axis=-1)
        p = jax.nn.softmax(jnp.where(mask, logits, -jnp.inf), axis=-1).astype(vb.dtype)
        p_sel = p[..., :topk * MOBA_BLOCK].reshape(n, N_HEADS, Q_BLOCK, topk, MOBA_BLOCK)
        p_own = p[..., topk * MOBA_BLOCK:]
        return (jnp.einsum('bhqnj,bhqnjd->bhqd', p_sel, v_sel)
                + jnp.einsum('bhqj,bhjd->bhqd', p_own, v_own))

    out = lax.map(attend_block, (q_blocks, idx_blocks, jnp.arange(nq)))
    return out.transpose(1, 0, 3, 2, 4).reshape(n, s, ATTN_WIDTH)


def moba_sample(q, k_new, v_new, cache_k, cache_v, page_table):
    n, t = q.shape[0], q.shape[1]
    n_pages = PAST_LEN // PAGE_SIZE
    total = PAST_LEN + t
    nb = -(-total // MOBA_BLOCK)
    topk = min(MOBA_TOPK, nb)
    qpos = PAST_LEN + np.arange(t)
    ob = qpos // MOBA_BLOCK
    page_sums = jnp.sum(cache_k[page_table], axis=2, dtype=jnp.float32)
    contrib = jnp.concatenate([page_sums, k_new.astype(jnp.float32)], axis=1)
    seg = np.concatenate([np.arange(n_pages) * PAGE_SIZE // MOBA_BLOCK, qpos // MOBA_BLOCK])
    block_sums = jax.ops.segment_sum(jnp.moveaxis(contrib, 1, 0), jnp.asarray(seg, jnp.int32),
                                     num_segments=nb, indices_are_sorted=True)
    kbar = jnp.moveaxis(block_sums, 0, 2) / MOBA_BLOCK
    qh = q.transpose(0, 2, 1, 3)
    gate = jnp.einsum('bhtd,bhnd->bhtn', qh.astype(jnp.float32), kbar)
    fully_past = np.arange(nb)[None, :] < ob[:, None]
    gate = jnp.where(fully_past, gate, -jnp.inf)
    _, idx = lax.top_k(gate, topk)
    sel_pos = idx[..., None] * MOBA_BLOCK + jnp.arange(MOBA_BLOCK)
    mask_sel = jnp.broadcast_to((idx < jnp.asarray(ob, jnp.int32)[None, None, :, None])[..., None], sel_pos.shape)
    own_np = ob[:, None] * MOBA_BLOCK + np.arange(MOBA_BLOCK)
    own_pos = jnp.broadcast_to(jnp.asarray(own_np, jnp.int32)[None, None, :, None, :], (n, N_HEADS, t, 1, MOBA_BLOCK))
    mask_own = jnp.broadcast_to(jnp.asarray(own_np <= qpos[:, None])[None, None, :, None, :], own_pos.shape)
    m = (topk + 1) * MOBA_BLOCK
    pos = jnp.concatenate([sel_pos, own_pos], axis=3).reshape(n, N_HEADS, t, m)
    mask = jnp.concatenate([mask_sel, mask_own], axis=3).reshape(n, N_HEADS, t, m)
    b_i = jnp.arange(n)[:, None, None, None]
    h_i = jnp.arange(N_HEADS)[None, :, None, None]
    in_cache = pos < PAST_LEN
    pc = jnp.minimum(pos, PAST_LEN - 1)
    phys = page_table[b_i, pc // PAGE_SIZE]
    slot = pc % PAGE_SIZE
    pn = jnp.clip(pos - PAST_LEN, 0, t - 1)
    k_rows = jnp.where(in_cache[..., None], cache_k[phys, slot, h_i], k_new[b_i, pn, h_i])
    v_rows = jnp.where(in_cache[..., None], cache_v[phys, slot, h_i], v_new[b_i, pn, h_i])
    logits = jnp.einsum('bhtd,bhtmd->bhtm', qh, k_rows, preferred_element_type=jnp.float32) * (HEAD_DIM ** -0.5)
    p = jax.nn.softmax(jnp.where(mask, logits, -jnp.inf), axis=-1).astype(v_rows.dtype)
    out = jnp.einsum('bhtm,bhtmd->bhtd', p, v_rows)
    return out.transpose(0, 2, 1, 3).reshape(n, t, ATTN_WIDTH)


def merge_branches(ya, yb, ga, gb, w_ba, w_bb, w_out):
    return (jax.nn.sigmoid(ga) * (ya @ w_ba) + jax.nn.sigmoid(gb) * (yb @ w_bb)) @ w_out


def conv_ffn(h, hist, w_up, conv_w, conv_b, w_down):
    up = h @ w_up
    t = up.shape[1]
    ext = jnp.concatenate([hist.astype(up.dtype), up], axis=1)
    c = conv_b
    for j in range(CONV_W):
        c = c + conv_w[j] * ext[:, j:j + t]
    g, z = jnp.split(c, 2, axis=-1)
    return (jax.nn.gelu(g) * z) @ w_down, ext[:, t:]


def setup_inputs(seed: int = 0) -> dict:
    key = jax.random.key(seed)
    ks = jax.random.split(key, 21)
    f32 = jnp.float32
    n_pages = PAST_LEN // PAGE_SIZE
    n_pool = (5 * DEC_BATCH * n_pages + 3) // 4
    dense = lambda k, shape, fan_in: jax.random.normal(k, shape, f32) * fan_in ** -0.5
    gain = lambda k, shape: 1.0 + 0.02 * jax.random.normal(k, shape, f32)
    page_table = jax.random.permutation(ks[5], n_pool)[:DEC_BATCH * n_pages]
    page_table = page_table.reshape(DEC_BATCH, n_pages).astype(jnp.int32)
    return {
        'x_prompt': jax.random.normal(ks[0], (BATCH, SEQ, D_MODEL), f32),
        'x_sample': jax.random.normal(ks[1], (DEC_BATCH, DEC_SEQ, D_MODEL), f32),
        'cache_k': jax.random.normal(ks[2], (DEPTH, n_pool, PAGE_SIZE, N_HEADS, HEAD_DIM), f32),
        'cache_v': jax.random.normal(ks[3], (DEPTH, n_pool, PAGE_SIZE, N_HEADS, HEAD_DIM), f32),
        'state_conv': jax.random.normal(ks[4], (DEPTH, DEC_BATCH, CONV_W - 1, 2 * D_FF), f32),
        'page_table': page_table,
        'norm1_w': gain(ks[6], (DEPTH, D_MODEL)),
        'w_in': dense(ks[7], (DEPTH, D_MODEL, IN_WIDTH), D_MODEL),
        'gmlp_vnorm_w': gain(ks[8], (DEPTH, GMLP_WIDTH)),
        'gmlp_ws': dense(ks[9], (DEPTH, GMLP_GROUPS, CHUNK, CHUNK), CHUNK),
        'gmlp_bs': gain(ks[10], (DEPTH, GMLP_GROUPS, CHUNK)),
        'q_norm_w': gain(ks[11], (DEPTH, HEAD_DIM)),
        'k_norm_w': gain(ks[12], (DEPTH, HEAD_DIM)),
        'w_branch_a': dense(ks[13], (DEPTH, GMLP_WIDTH, D_MODEL), GMLP_WIDTH),
        'w_branch_b': dense(ks[14], (DEPTH, ATTN_WIDTH, D_MODEL), ATTN_WIDTH),
        'w_out': dense(ks[15], (DEPTH, D_MODEL, D_MODEL), D_MODEL),
        'norm2_w': gain(ks[16], (DEPTH, D_MODEL)),
        'w_up': dense(ks[17], (DEPTH, D_MODEL, 2 * D_FF), D_MODEL),
        'conv_w': dense(ks[18], (DEPTH, CONV_W, 2 * D_FF), CONV_W),
        'conv_b': 0.02 * jax.random.normal(ks[19], (DEPTH, 2 * D_FF), f32),
        'w_down': dense(ks[20], (DEPTH, D_FF, D_MODEL), D_FF),
    }


def reference(x_prompt, x_sample, cache_k, cache_v, state_conv, page_table,
              norm1_w, w_in, gmlp_vnorm_w, gmlp_ws, gmlp_bs, q_norm_w, k_norm_w,
              w_branch_a, w_branch_b, w_out, norm2_w, w_up, conv_w, conv_b, w_down):
    xp, xs = x_prompt, x_sample
    k_p, v_p, k_s, v_s, gv_s, cv_p, cv_s = [], [], [], [], [], [], []
    for l in range(DEPTH):
        hp = rmsnorm(xp, norm1_w[l])
        u, v, q, k, vv, ga, gb = mixer_inputs(hp, w_in[l], gmlp_vnorm_w[l], q_norm_w[l], k_norm_w[l])
        ya = gmlp_prompt(u, v, gmlp_ws[l], gmlp_bs[l])
        yb = moba_prompt(q, k, vv)
        xp = xp + merge_branches(ya, yb, ga, gb, w_branch_a[l], w_branch_b[l], w_out[l])
        hist0 = jnp.zeros((xp.shape[0], CONV_W - 1, 2 * D_FF), xp.dtype)
        f, cst = conv_ffn(rmsnorm(xp, norm2_w[l]), hist0, w_up[l], conv_w[l], conv_b[l], w_down[l])
        xp = xp + f
        k_p.append(k)
        v_p.append(vv)
        cv_p.append(cst)
        hs = rmsnorm(xs, norm1_w[l])
        u, v, q, k, vv, ga, gb = mixer_inputs(hs, w_in[l], gmlp_vnorm_w[l], q_norm_w[l], k_norm_w[l])
        ya = gmlp_sample(u, v, gmlp_ws[l], gmlp_bs[l])
        yb = moba_sample(q, k, vv, cache_k[l], cache_v[l], page_table)
        xs = xs + merge_branches(ya, yb, ga, gb, w_branch_a[l], w_branch_b[l], w_out[l])
        f, cst = conv_ffn(rmsnorm(xs, norm2_w[l]), state_conv[l], w_up[l], conv_w[l], conv_b[l], w_down[l])
        xs = xs + f
        k_s.append(k)
        v_s.append(vv)
        gv_s.append(v)
        cv_s.append(cst)
    return (xp, xs, jnp.stack(k_p), jnp.stack(v_p), jnp.stack(k_s), jnp.stack(v_s),
            jnp.stack(gv_s), jnp.stack(cv_p), jnp.stack(cv_s))
```

```python
import functools

import jax
import jax.numpy as jnp
from jax import lax
from jax.experimental import pallas as pl
from jax.experimental.pallas import tpu as pltpu

F32 = jnp.float32
BF16 = jnp.bfloat16

D_MODEL = 1024
N_HEADS = 8
HEAD_DIM = 128
GMLP_GROUPS = 8
GROUP_DIM = 128
CHUNK = 128
MOBA_BLOCK = 256
MOBA_TOPK = 3
D_FF = 2816
CONV_W = 3
EPS = 1e-6
PAGE_SIZE = 128
N_SEG = 7
FF_CHUNK = 256
N_FF_CHUNKS = D_FF // FF_CHUNK
PAGES_PER_STEP = 16
ROW_TILE = 512
VMEM_LIMIT = 60 * 1024 * 1024


def _rms(x, w):
    ms = jnp.mean(x * x, axis=-1, keepdims=True)
    return x * lax.rsqrt(ms + EPS) * w


def _const_spec(shape):
    nd = len(shape)
    return pl.BlockSpec(shape, lambda *_: (0,) * nd, pipeline_mode=pl.Buffered(1))


def _params(*sem):
    return pltpu.CompilerParams(dimension_semantics=sem, vmem_limit_bytes=VMEM_LIMIT)


def _top3_select(gate, lane, n_valid):
    neg = jnp.float32(-jnp.inf)
    gate = jnp.where(lane < n_valid, gate, neg)
    picks = []
    for _ in range(MOBA_TOPK):
        m = jnp.max(gate, axis=-1, keepdims=True)
        first = jnp.min(jnp.where(gate == m, lane, jnp.int32(2 ** 30)), axis=-1, keepdims=True)
        picks.append(first)
        gate = jnp.where(lane == first, neg, gate)
    return picks


def _prompt_in_kernel(x_ref, n1_ref, w_ref, vn_ref, qn_ref, kn_ref, ws_ref, bst_ref,
                      ya_ref, q_ref, k_ref, vv_ref, kb_ref, vb_ref, ksum_ref, sga_ref, sgb_ref):
    tm = x_ref.shape[0]
    h = _rms(x_ref[...], n1_ref[...]).astype(BF16)

    def seg(s):
        return jnp.dot(h, w_ref[:, s * D_MODEL:(s + 1) * D_MODEL], preferred_element_type=F32)

    u = jax.nn.gelu(seg(0))
    v = _rms(jax.nn.gelu(seg(1)), vn_ref[...]).astype(BF16)
    row = lax.broadcasted_iota(jnp.int32, (CHUNK, CHUNK), 0)
    col = lax.broadcasted_iota(jnp.int32, (CHUNK, CHUNK), 1)
    for g in range(GMLP_GROUPS):
        wg = jnp.where(row >= col, ws_ref[g], 0.0).astype(BF16)
        bg = bst_ref[:, g:g + 1]
        cs = slice(g * GROUP_DIM, (g + 1) * GROUP_DIM)
        for c in range(tm // CHUNK):
            rs = slice(c * CHUNK, (c + 1) * CHUNK)
            mixed = jnp.dot(wg, v[rs, cs], preferred_element_type=F32) + bg
            ya_ref[rs, cs] = (u[rs, cs] * mixed).astype(BF16)

    aq = seg(2)
    for hd in range(N_HEADS):
        cs = slice(hd * HEAD_DIM, (hd + 1) * HEAD_DIM)
        q_ref[:, cs] = _rms(aq[:, cs], qn_ref[...])
    ak = seg(3)
    for hd in range(N_HEADS):
        cs = slice(hd * HEAD_DIM, (hd + 1) * HEAD_DIM)
        kh = _rms(ak[:, cs], kn_ref[...])
        k_ref[:, cs] = kh
        kb_ref[:, cs] = kh.astype(BF16)
    for blk in range(tm // MOBA_BLOCK):
        rs = slice(blk * MOBA_BLOCK, (blk + 1) * MOBA_BLOCK)
        ksum_ref[0, blk:blk + 1, :] = jnp.sum(k_ref[rs, :], axis=0, keepdims=True)
    av = seg(4)
    vv_ref[...] = av
    vb_ref[...] = av.astype(BF16)
    sga_ref[...] = jax.nn.sigmoid(seg(5))
    sgb_ref[...] = jax.nn.sigmoid(seg(6))


def _prompt_in(x, n1, w_in, vn, qn, kn, ws, bst):
    n = x.shape[0]
    tm = ROW_TILE
    nt = n // tm
    row_spec = pl.BlockSpec((tm, D_MODEL), lambda i: (i, 0))
    f32_out = jax.ShapeDtypeStruct((n, D_MODEL), F32)
    bf_out = jax.ShapeDtypeStruct((n, D_MODEL), BF16)
    return pl.pallas_call(
        _prompt_in_kernel,
        grid=(nt,),
        in_specs=[row_spec, _const_spec((1, D_MODEL)), _const_spec((D_MODEL, N_SEG * D_MODEL)),
                  _const_spec((1, D_MODEL)), _const_spec((1, HEAD_DIM)), _const_spec((1, HEAD_DIM)),
                  _const_spec((GMLP_GROUPS, CHUNK, CHUNK)), _const_spec((CHUNK, GMLP_GROUPS))],
        out_specs=[row_spec, row_spec, row_spec, row_spec, row_spec, row_spec,
                   pl.BlockSpec((1, tm // MOBA_BLOCK, D_MODEL), lambda i: (i, 0, 0)),
                   row_spec, row_spec],
        out_shape=[bf_out, f32_out, f32_out, f32_out, bf_out, bf_out,
                   jax.ShapeDtypeStruct((nt, tm // MOBA_BLOCK, D_MODEL), F32),
                   f32_out, f32_out],
        compiler_params=_params("parallel"),
        name="prompt_in",
    )(x, n1, w_in, vn, qn, kn, ws, bst)


def _prompt_attn_kernel(q_ref, kb_ref, vb_ref, ksum_ref, o_ref):
    i = pl.program_id(2)
    nb = ksum_ref.shape[1]
    q = q_ref[...]
    kbar = ksum_ref[0] * (1.0 / MOBA_BLOCK)
    gate = lax.dot_general(q, kbar, (((1,), (1,)), ((), ())), precision=lax.Precision.HIGHEST,
                           preferred_element_type=F32)
    lane = lax.broadcasted_iota(jnp.int32, (MOBA_BLOCK, nb), 1)
    picks = _top3_select(gate, lane, i)
    sel = jnp.zeros((MOBA_BLOCK, nb), F32)
    for first in picks:
        sel = jnp.where((lane == first) & (lane < i), 1.0, sel)

    scale = HEAD_DIM ** -0.5
    qb = q.astype(BF16)
    neg = jnp.float32(-jnp.inf)

    def scores(j):
        kj = kb_ref[pl.ds(pl.multiple_of(j * MOBA_BLOCK, MOBA_BLOCK), MOBA_BLOCK), :]
        return lax.dot_general(qb, kj, (((1,), (1,)), ((), ())), preferred_element_type=F32) * scale

    def values(j):
        return vb_ref[pl.ds(pl.multiple_of(j * MOBA_BLOCK, MOBA_BLOCK), MOBA_BLOCK), :]

    r = lax.broadcasted_iota(jnp.int32, (MOBA_BLOCK, MOBA_BLOCK), 0)
    c = lax.broadcasted_iota(jnp.int32, (MOBA_BLOCK, MOBA_BLOCK), 1)
    s = jnp.where(c <= r, scores(i), neg)
    m = jnp.max(s, axis=-1, keepdims=True)
    p = jnp.exp(s - m)
    l = jnp.sum(p, axis=-1, keepdims=True)
    acc = jnp.dot(p.astype(BF16), values(i), preferred_element_type=F32)

    def body(j, carry):
        m, l, acc = carry
        chosen = jnp.sum(jnp.where(lane == j, sel, 0.0), axis=-1, keepdims=True) > 0.5
        s = jnp.where(chosen, scores(j), neg)
        m_new = jnp.maximum(m, jnp.max(s, axis=-1, keepdims=True))
        alpha = jnp.exp(m - m_new)
        p = jnp.exp(s - m_new)
        l = alpha * l + jnp.sum(p, axis=-1, keepdims=True)
        acc = alpha * acc + jnp.dot(p.astype(BF16), values(j), preferred_element_type=F32)
        return m_new, l, acc

    m, l, acc = lax.fori_loop(0, i, body, (m, l, acc))
    o_ref[...] = (acc / l).astype(o_ref.dtype)


def _prompt_attn(q, kb, vb, ksum, batch, seq):
    n = q.shape[0]
    nb = seq // MOBA_BLOCK
    ksum = ksum.reshape(batch, nb, D_MODEL)
    return pl.pallas_call(
        _prompt_attn_kernel,
        grid=(batch, N_HEADS, nb),
        in_specs=[pl.BlockSpec((MOBA_BLOCK, HEAD_DIM), lambda b, h, i: (b * nb + i, h)),
                  pl.BlockSpec((seq, HEAD_DIM), lambda b, h, i: (b, h)),
                  pl.BlockSpec((seq, HEAD_DIM), lambda b, h, i: (b, h)),
                  pl.BlockSpec((1, nb, HEAD_DIM), lambda b, h, i: (b, 0, h))],
        out_specs=pl.BlockSpec((MOBA_BLOCK, HEAD_DIM), lambda b, h, i: (b * nb + i, h)),
        out_shape=jax.ShapeDtypeStruct((n, D_MODEL), BF16),
        compiler_params=_params("parallel", "parallel", "arbitrary"),
        name="prompt_attn",
    )(q, kb, vb, ksum)


def _merge_kernel(x_ref, ya_ref, yb_ref, sga_ref, sgb_ref, wba_ref, wbb_ref, wout_ref, o_ref):
    a = jnp.dot(ya_ref[...], wba_ref[...], preferred_element_type=F32)
    b = jnp.dot(yb_ref[...], wbb_ref[...], preferred_element_type=F32)
    mix = (sga_ref[...] * a + sgb_ref[...] * b).astype(BF16)
    o_ref[...] = x_ref[...] + jnp.dot(mix, wout_ref[...], preferred_element_type=F32)


def _merge(x, ya, yb, sga, sgb, wba, wbb, wout, tm):
    n = x.shape[0]
    row_spec = pl.BlockSpec((tm, D_MODEL), lambda i: (i, 0))
    w_spec = _const_spec((D_MODEL, D_MODEL))
    return pl.pallas_call(
        _merge_kernel,
        grid=(n // tm,),
        in_specs=[row_spec] * 5 + [w_spec] * 3,
        out_specs=row_spec,
        out_shape=jax.ShapeDtypeStruct((n, D_MODEL), F32),
        compiler_params=_params("parallel"),
        name="merge",
    )(x, ya, yb, sga, sgb, wba, wbb, wout)


def _prompt_ffn_kernel(x_ref, n2_ref, wu_ref, cw_ref, cb_ref, wd_ref,
                       o_ref, st_ref, h_scr, acc_scr, carry_scr, *, tiles_per_seq):
    tm = x_ref.shape[0]

    @pl.when(pl.program_id(0) % tiles_per_seq == 0)
    def _():
        carry_scr[...] = jnp.zeros_like(carry_scr)

    x = x_ref[...]
    h_scr[...] = _rms(x, n2_ref[...]).astype(BF16)
    acc_scr[...] = x
    row = lax.broadcasted_iota(jnp.int32, (tm, FF_CHUNK), 0)

    def conv(part, c):
        up = jnp.dot(h_scr[...], wu_ref[part, c], preferred_element_type=F32)
        prev = carry_scr[part, c]
        u1 = jnp.where(row == 0, prev[1:2, :], pltpu.roll(up, 1, 0))
        u2 = jnp.where(row == 0, prev[0:1, :], jnp.where(row == 1, prev[1:2, :], pltpu.roll(up, 2, 0)))
        carry_scr[part, c] = up[tm - 2:tm, :]
        cw = cw_ref[part, c]
        return cb_ref[part, c] + cw[0:1, :] * u2 + cw[1:2, :] * u1 + cw[2:3, :] * up

    def chunk(c, _):
        act = (jax.nn.gelu(conv(0, c)) * conv(1, c)).astype(BF16)
        acc_scr[...] += jnp.dot(act, wd_ref[c], preferred_element_type=F32)
        return 0

    lax.fori_loop(0, N_FF_CHUNKS, chunk, 0)
    o_ref[...] = acc_scr[...]
    st_ref[0] = carry_scr[...]


def _prompt_ffn(x, n2, wu_c, cw_c, cb_c, wd_c, batch, seq):
    n = x.shape[0]
    tm = ROW_TILE
    tiles_per_seq = seq // tm
    row_spec = pl.BlockSpec((tm, D_MODEL), lambda i: (i, 0))
    st_shape = (2, N_FF_CHUNKS, CONV_W - 1, FF_CHUNK)
    return pl.pallas_call(
        functools.partial(_prompt_ffn_kernel, tiles_per_seq=tiles_per_seq),
        grid=(n // tm,),
        in_specs=[row_spec, _const_spec((1, D_MODEL)),
                  _const_spec((2, N_FF_CHUNKS, D_MODEL, FF_CHUNK)),
                  _const_spec((2, N_FF_CHUNKS, CONV_W, FF_CHUNK)),
                  _const_spec((2, N_FF_CHUNKS, 1, FF_CHUNK)),
                  _const_spec((N_FF_CHUNKS, FF_CHUNK, D_MODEL))],
        out_specs=[row_spec,
                   pl.BlockSpec((1,) + st_shape, lambda i: (i // tiles_per_seq, 0, 0, 0, 0))],
        out_shape=[jax.ShapeDtypeStruct((n, D_MODEL), F32),
                   jax.ShapeDtypeStruct((batch,) + st_shape, F32)],
        scratch_shapes=[pltpu.VMEM((tm, D_MODEL), BF16), pltpu.VMEM((tm, D_MODEL), F32),
                        pltpu.VMEM(st_shape, F32)],
        compiler_params=_params("arbitrary"),
        name="prompt_ffn",
    )(x, n2, wu_c, cw_c, cb_c, wd_c)


def _sample_in_kernel(x_ref, n1_ref, w_ref, vn_ref, qn_ref, kn_ref, wd0_ref, b0_ref,
                      ya_ref, q_ref, k_ref, vv_ref, gv_ref, sga_ref, sgb_ref):
    h = _rms(x_ref[...], n1_ref[...]).astype(BF16)

    def seg(s):
        return jnp.dot(h, w_ref[:, s * D_MODEL:(s + 1) * D_MODEL], preferred_element_type=F32)

    u = jax.nn.gelu(seg(0))
    v = _rms(jax.nn.gelu(seg(1)), vn_ref[...])
    gv_ref[...] = v
    ya_ref[...] = (u * (wd0_ref[...] * v + b0_ref[...])).astype(BF16)
    aq = seg(2)
    ak = seg(3)
    for hd in range(N_HEADS):
        cs = slice(hd * HEAD_DIM, (hd + 1) * HEAD_DIM)
        q_ref[:, cs] = _rms(aq[:, cs], qn_ref[...])
        k_ref[:, cs] = _rms(ak[:, cs], kn_ref[...])
    vv_ref[...] = seg(4)
    sga_ref[...] = jax.nn.sigmoid(seg(5))
    sgb_ref[...] = jax.nn.sigmoid(seg(6))


def _sample_in(x, n1, w_in, vn, qn, kn, wd0, b0):
    n = x.shape[0]
    f32_out = jax.ShapeDtypeStruct((n, D_MODEL), F32)
    return pl.pallas_call(
        _sample_in_kernel,
        out_shape=[jax.ShapeDtypeStruct((n, D_MODEL), BF16)] + [f32_out] * 6,
        compiler_params=pltpu.CompilerParams(vmem_limit_bytes=VMEM_LIMIT),
        name="sample_in",
    )(x, n1, w_in, vn, qn, kn, wd0, b0)


def _block_sum_kernel(pt_ref, *refs):
    pages, out_ref = refs[:PAGES_PER_STEP], refs[PAGES_PER_STEP]
    for r in range(PAGES_PER_STEP // 2):
        s = (jnp.sum(pages[2 * r][0, 0], axis=0, keepdims=True)
             + jnp.sum(pages[2 * r + 1][0, 0], axis=0, keepdims=True))
        out_ref[0, 0, r:r + 1, :] = s


def _block_sums(cache_k, page_table_flat, n_seq, n_pages):
    depth = cache_k.shape[0]
    steps = n_pages // PAGES_PER_STEP
    blocks_per_step = PAGES_PER_STEP * PAGE_SIZE // MOBA_BLOCK

    def page_spec(r):
        return pl.BlockSpec((1, 1, PAGE_SIZE, D_MODEL),
                            lambda l, b, g, pt: (l, pt[b * n_pages + g * PAGES_PER_STEP + r], 0, 0))

    return pl.pallas_call(
        _block_sum_kernel,
        grid_spec=pltpu.PrefetchScalarGridSpec(
            num_scalar_prefetch=1,
            grid=(depth, n_seq, steps),
            in_specs=[page_spec(r) for r in range(PAGES_PER_STEP)],
            out_specs=pl.BlockSpec((1, 1, blocks_per_step, D_MODEL), lambda l, b, g, pt: (l, b, g, 0)),
        ),
        out_shape=jax.ShapeDtypeStruct((depth, n_seq, steps * blocks_per_step, D_MODEL), F32),
        compiler_params=_params("parallel", "parallel", "arbitrary"),
        name="cache_block_sums",
    )(page_table_flat, *([cache_k] * PAGES_PER_STEP))


def _sample_gate_kernel(q_ref, k_ref, bsum_ref, idx_ref, prod_scr, *, n_past):
    q = q_ref[0]
    inv = 1.0 / MOBA_BLOCK
    prod_scr[...] = jnp.zeros_like(prod_scr)
    prod_scr[0:n_past, :] = (bsum_ref[0, 0] * inv) * q
    prod_scr[n_past:n_past + 1, :] = (k_ref[0] * inv) * q
    head = lax.broadcasted_iota(jnp.int32, (N_HEADS, D_MODEL), 0)
    colh = lax.broadcasted_iota(jnp.int32, (N_HEADS, D_MODEL), 1) // HEAD_DIM
    ind = jnp.where(head == colh, 1.0, 0.0).astype(F32)
    gate = lax.dot_general(ind, prod_scr[...], (((1,), (1,)), ((), ())),
                           precision=lax.Precision.HIGHEST, preferred_element_type=F32)
    lane = lax.broadcasted_iota(jnp.int32, gate.shape, 1)
    picks = _top3_select(gate, lane, n_past)
    out = jnp.zeros(gate.shape, jnp.int32)
    for r, first in enumerate(picks):
        out = jnp.where(lane == r, first, out)
    idx_ref[0] = out


def _sample_gate(q, k, bsum, layer):
    n = q.shape[0]
    n_past = bsum.shape[2]
    lanes = 128
    assert n_past + 1 <= lanes
    return pl.pallas_call(
        functools.partial(_sample_gate_kernel, n_past=n_past),
        grid=(n,),
        in_specs=[pl.BlockSpec((1, 1, D_MODEL), lambda b: (b, 0, 0)),
                  pl.BlockSpec((1, 1, D_MODEL), lambda b: (b, 0, 0)),
                  pl.BlockSpec((1, 1, n_past, D_MODEL), lambda b: (layer, b, 0, 0))],
        out_specs=pl.BlockSpec((1, N_HEADS, lanes), lambda b: (b, 0, 0)),
        out_shape=jax.ShapeDtypeStruct((n, N_HEADS, lanes), jnp.int32),
        scratch_shapes=[pltpu.VMEM((lanes, D_MODEL), F32)],
        compiler_params=_params("parallel"),
        name="sample_gate",
    )(q.reshape(n, 1, D_MODEL), k.reshape(n, 1, D_MODEL), bsum)


N_SEL_PAGES = MOBA_TOPK * MOBA_BLOCK // PAGE_SIZE


def _sample_attn_kernel(pt_ref, idx_ref, q_ref, kn_ref, vn_ref, *refs):
    k_pages, v_pages = refs[:N_SEL_PAGES], refs[N_SEL_PAGES:2 * N_SEL_PAGES]
    o_ref = refs[2 * N_SEL_PAGES]
    scale = HEAD_DIM ** -0.5
    q = q_ref[0, 0]
    qb = jnp.broadcast_to(q, (8, HEAD_DIM)).astype(BF16)
    s_pages = [lax.dot_general(qb, kp[0, 0].astype(BF16), (((1,), (1,)), ((), ())),
                               preferred_element_type=F32)[0:1, :] * scale for kp in k_pages]
    kn = kn_ref[0, 0]
    s_own = jnp.sum(q.astype(BF16).astype(F32) * kn.astype(BF16).astype(F32), axis=-1, keepdims=True) * scale
    m = s_own
    for s in s_pages:
        m = jnp.maximum(m, jnp.max(s, axis=-1, keepdims=True))
    p_own = jnp.exp(s_own - m)
    l = p_own
    acc = p_own.astype(BF16).astype(F32) * vn_ref[0, 0].astype(BF16).astype(F32)
    for s, vp in zip(s_pages, v_pages):
        p = jnp.exp(s - m)
        l = l + jnp.sum(p, axis=-1, keepdims=True)
        pb = jnp.broadcast_to(p, (8, PAGE_SIZE)).astype(BF16)
        acc = acc + jnp.dot(pb, vp[0, 0].astype(BF16), preferred_element_type=F32)[0:1, :]
    o_ref[0, 0] = (acc / l).astype(o_ref.dtype)


def _sample_attn(q, k_new, v_new, cache_k, cache_v, page_table_flat, idx_flat, layer, n_pages):
    n = q.shape[0]
    pages_per_block = MOBA_BLOCK // PAGE_SIZE

    def page_spec(j):
        r, p = divmod(j, pages_per_block)

        def index_map(b, h, pt, ix):
            blk = ix[(b * N_HEADS + h) * MOBA_TOPK + r]
            return (layer, pt[b * n_pages + blk * pages_per_block + p], 0, h)

        return pl.BlockSpec((1, 1, PAGE_SIZE, HEAD_DIM), index_map)

    vec_spec = pl.BlockSpec((1, 1, 1, HEAD_DIM), lambda b, h, pt, ix: (b, h, 0, 0))
    as_heads = lambda a: a.reshape(n, N_HEADS, 1, HEAD_DIM)
    out = pl.pallas_call(
        _sample_attn_kernel,
        grid_spec=pltpu.PrefetchScalarGridSpec(
            num_scalar_prefetch=2,
            grid=(n, N_HEADS),
            in_specs=[vec_spec] * 3 + [page_spec(j) for j in range(N_SEL_PAGES)] * 2,
            out_specs=vec_spec,
        ),
        out_shape=jax.ShapeDtypeStruct((n, N_HEADS, 1, HEAD_DIM), BF16),
        compiler_params=_params("parallel", "arbitrary"),
        name="sample_attn",
    )(page_table_flat, idx_flat, as_heads(q), as_heads(k_new), as_heads(v_new),
      *([cache_k] * N_SEL_PAGES), *([cache_v] * N_SEL_PAGES))
    return out.reshape(n, D_MODEL)


def _sample_out_kernel(x_ref, ya_ref, yb_ref, sga_ref, sgb_ref, wba_ref, wbb_ref, wout_ref,
                       n2_ref, wu_ref, cw_ref, cb_ref, wd_ref, h0_ref, h1_ref, o_ref, up_ref):
    a = jnp.dot(ya_ref[...], wba_ref[...], preferred_element_type=F32)
    b = jnp.dot(yb_ref[...], wbb_ref[...], preferred_element_type=F32)
    mix = (sga_ref[...] * a + sgb_ref[...] * b).astype(BF16)
    x = x_ref[...] + jnp.dot(mix, wout_ref[...], preferred_element_type=F32)
    h = _rms(x, n2_ref[...]).astype(BF16)
    up = jnp.dot(h, wu_ref[...], preferred_element_type=F32)
    up_ref[...] = up
    c = cb_ref[...] + cw_ref[0:1, :] * h0_ref[...] + cw_ref[1:2, :] * h1_ref[...] + cw_ref[2:3, :] * up
    act = (jax.nn.gelu(c[:, :D_FF]) * c[:, D_FF:]).astype(BF16)
    o_ref[...] = x + jnp.dot(act, wd_ref[...], preferred_element_type=F32)


def _sample_out(x, ya, yb, sga, sgb, wba, wbb, wout, n2, wu, cw, cb, wd, h0, h1):
    n = x.shape[0]
    return pl.pallas_call(
        _sample_out_kernel,
        out_shape=[jax.ShapeDtypeStruct((n, D_MODEL), F32), jax.ShapeDtypeStruct((n, 2 * D_FF), F32)],
        compiler_params=pltpu.CompilerParams(vmem_limit_bytes=VMEM_LIMIT),
        name="sample_out",
    )(x, ya, yb, sga, sgb, wba, wbb, wout, n2, wu, cw, cb, wd, h0, h1)


def _chunk_cols(a):
    lead = a.shape[:-1]
    a = a.reshape(lead + (2, N_FF_CHUNKS, FF_CHUNK))
    return jnp.moveaxis(a, (-3, -2), (0, 1))


def kernel(x_prompt, x_sample, cache_k, cache_v, state_conv, page_table, norm1_w, w_in, gmlp_vnorm_w, gmlp_ws, gmlp_bs, q_norm_w, k_norm_w, w_branch_a, w_branch_b, w_out, norm2_w, w_up, conv_w, conv_b, w_down):
    batch, seq, _ = x_prompt.shape
    n_seq, dec_seq, _ = x_sample.shape
    depth = w_in.shape[0]
    n_pages = page_table.shape[1]
    assert dec_seq == 1 and seq % ROW_TILE == 0 and ROW_TILE % MOBA_BLOCK == 0
    assert n_pages % PAGES_PER_STEP == 0 and (n_pages * PAGE_SIZE) % MOBA_BLOCK == 0

    n_pool = cache_k.shape[1]
    ck = cache_k.reshape(depth, n_pool, PAGE_SIZE, D_MODEL)
    cv = cache_v.reshape(depth, n_pool, PAGE_SIZE, D_MODEL)
    pt_flat = page_table.reshape(-1)
    bsum = _block_sums(ck, pt_flat, n_seq, n_pages)

    xp = x_prompt.reshape(batch * seq, D_MODEL)
    xs = x_sample.reshape(n_seq, D_MODEL)
    row = lambda a: a.reshape(1, -1)
    k_p, v_p, k_s, v_s, gv_s, cv_p, cv_s = [], [], [], [], [], [], []
    for l in range(depth):
        w_in_b = w_in[l].astype(BF16)
        wba, wbb, wout = w_branch_a[l].astype(BF16), w_branch_b[l].astype(BF16), w_out[l].astype(BF16)
        wu_b, wd_b = w_up[l].astype(BF16), w_down[l].astype(BF16)
        n1, n2, vn, qn, kn = row(norm1_w[l]), row(norm2_w[l]), row(gmlp_vnorm_w[l]), row(q_norm_w[l]), row(k_norm_w[l])

        ya, q, k, vv, kb, vb, ksum, sga, sgb = _prompt_in(xp, n1, w_in_b, vn, qn, kn, gmlp_ws[l], gmlp_bs[l].T)
        yb = _prompt_attn(q, kb, vb, ksum, batch, seq)
        xp = _merge(xp, ya, yb, sga, sgb, wba, wbb, wout, ROW_TILE)
        xp, st = _prompt_ffn(xp, n2, _chunk_cols(wu_b), _chunk_cols(conv_w[l]),
                             _chunk_cols(conv_b[l].reshape(1, -1)),
                             wd_b.reshape(N_FF_CHUNKS, FF_CHUNK, D_MODEL), batch, seq)
        k_p.append(k.reshape(batch, seq, N_HEADS, HEAD_DIM))
        v_p.append(vv.reshape(batch, seq, N_HEADS, HEAD_DIM))
        cv_p.append(jnp.transpose(st, (0, 3, 1, 2, 4)).reshape(batch, CONV_W - 1, 2 * D_FF))

        wd0 = jnp.repeat(gmlp_ws[l][:, 0, 0], GROUP_DIM).reshape(1, -1)
        b0 = jnp.repeat(gmlp_bs[l][:, 0], GROUP_DIM).reshape(1, -1)
        ya, q, k, vv, gv, sga, sgb = _sample_in(xs, n1, w_in_b, vn, qn, kn, wd0, b0)
        idx = _sample_gate(q, k, bsum, l)
        idx_flat = idx[:, :, :MOBA_TOPK].reshape(-1)
        yb = _sample_attn(q, k, vv, ck, cv, pt_flat, idx_flat, l, n_pages)
        hist = state_conv[l]
        xs, up = _sample_out(xs, ya, yb, sga, sgb, wba, wbb, wout, n2, wu_b, conv_w[l],
                             conv_b[l].reshape(1, -1), wd_b, hist[:, 0, :], hist[:, 1, :])
        k_s.append(k.reshape(n_seq, 1, N_HEADS, HEAD_DIM))
        v_s.append(vv.reshape(n_seq, 1, N_HEADS, HEAD_DIM))
        gv_s.append(gv.reshape(n_seq, 1, D_MODEL))
        cv_s.append(jnp.stack([hist[:, 1, :], up], axis=1))

    return (xp.reshape(batch, seq, D_MODEL), xs.reshape(n_seq, 1, D_MODEL),
            jnp.stack(k_p), jnp.stack(v_p), jnp.stack(k_s), jnp.stack(v_s),
            jnp.stack(gv_s), jnp.stack(cv_p), jnp.stack(cv_s))
```

```python
import functools

import jax
import jax.numpy as jnp
from jax import lax
from jax.experimental import pallas as pl
from jax.experimental.pallas import tpu as pltpu

F32 = jnp.float32
BF16 = jnp.bfloat16

D_MODEL = 1024
N_HEADS = 8
HEAD_DIM = 128
GMLP_GROUPS = 8
GROUP_DIM = 128
CHUNK = 128
MOBA_BLOCK = 256
MOBA_TOPK = 3
D_FF = 2816
CONV_W = 3
EPS = 1e-6
PAGE_SIZE = 128
N_SEG = 7
FF_CHUNK = 256
N_FF_CHUNKS = D_FF // FF_CHUNK
PAGES_PER_STEP = 16
PAGES_PER_BLOCK = MOBA_BLOCK // PAGE_SIZE
N_SEL_PAGES = MOBA_TOPK * PAGES_PER_BLOCK
ROW_TILE = 512
ATTN_HEADS = 4
LOG2_E = 1.4426950408889634
VMEM_LIMIT = 60 * 1024 * 1024


def _rms(x, w):
    ms = jnp.mean(x * x, axis=-1, keepdims=True)
    return x * lax.rsqrt(ms + EPS) * w


def _const_spec(shape):
    nd = len(shape)
    return pl.BlockSpec(shape, lambda *_: (0,) * nd, pipeline_mode=pl.Buffered(1))


def _params(*sem):
    return pltpu.CompilerParams(dimension_semantics=sem, vmem_limit_bytes=VMEM_LIMIT)


def _top3_select(gate, pos, n_valid):
    neg = jnp.float32(-jnp.inf)
    gate = jnp.where(pos < n_valid, gate, neg)
    picks = []
    for _ in range(MOBA_TOPK):
        m = jnp.max(gate, axis=0, keepdims=True)
        first = jnp.min(jnp.where(gate == m, pos, jnp.float32(2 ** 30)), axis=0, keepdims=True)
        picks.append(first)
        gate = jnp.where(pos == first, neg, gate)
    return picks


def _prompt_in_kernel(x_ref, n1_ref, w_ref, vn_ref, qn_ref, kn_ref, ws_ref, bst_ref,
                      ya_ref, q_ref, k_ref, vv_ref, kb_ref, vt_ref, ksum_ref, sga_ref, sgb_ref):
    tm = x_ref.shape[0]
    h = _rms(x_ref[...], n1_ref[...]).astype(BF16)

    def seg(s):
        return jnp.dot(h, w_ref[:, s * D_MODEL:(s + 1) * D_MODEL], preferred_element_type=F32)

    u = jax.nn.gelu(seg(0))
    v = _rms(jax.nn.gelu(seg(1)), vn_ref[...]).astype(BF16)
    row = lax.broadcasted_iota(jnp.int32, (CHUNK, CHUNK), 0)
    col = lax.broadcasted_iota(jnp.int32, (CHUNK, CHUNK), 1)
    for g in range(GMLP_GROUPS):
        wg = jnp.where(row >= col, ws_ref[g], 0.0).astype(BF16)
        bg = bst_ref[:, g:g + 1]
        cs = slice(g * GROUP_DIM, (g + 1) * GROUP_DIM)
        for c in range(tm // CHUNK):
            rs = slice(c * CHUNK, (c + 1) * CHUNK)
            mixed = jnp.dot(wg, v[rs, cs], preferred_element_type=F32) + bg
            ya_ref[rs, cs] = (u[rs, cs] * mixed).astype(BF16)

    aq = seg(2)
    for hd in range(N_HEADS):
        cs = slice(hd * HEAD_DIM, (hd + 1) * HEAD_DIM)
        q_ref[:, cs] = _rms(aq[:, cs], qn_ref[...])
    ak = seg(3)
    for hd in range(N_HEADS):
        cs = slice(hd * HEAD_DIM, (hd + 1) * HEAD_DIM)
        kh = _rms(ak[:, cs], kn_ref[...])
        k_ref[:, cs] = kh
        kb_ref[:, cs] = kh.astype(BF16)
    for blk in range(tm // MOBA_BLOCK):
        rs = slice(blk * MOBA_BLOCK, (blk + 1) * MOBA_BLOCK)
        ksum_ref[0, blk:blk + 1, :] = jnp.sum(k_ref[rs, :], axis=0, keepdims=True)
    av = seg(4)
    vv_ref[...] = av
    for blk in range(tm // MOBA_BLOCK):
        rs = slice(blk * MOBA_BLOCK, (blk + 1) * MOBA_BLOCK)
        for hd in range(N_HEADS):
            vt_ref[blk, hd] = av[rs, hd * HEAD_DIM:(hd + 1) * HEAD_DIM].T.astype(BF16)
    sga_ref[...] = jax.nn.sigmoid(seg(5))
    sgb_ref[...] = jax.nn.sigmoid(seg(6))


def _prompt_in(x, n1, w_in, vn, qn, kn, ws, bst):
    n = x.shape[0]
    tm = ROW_TILE
    nt = n // tm
    row_spec = pl.BlockSpec((tm, D_MODEL), lambda i: (i, 0))
    f32_out = jax.ShapeDtypeStruct((n, D_MODEL), F32)
    bf_out = jax.ShapeDtypeStruct((n, D_MODEL), BF16)
    return pl.pallas_call(
        _prompt_in_kernel,
        grid=(nt,),
        in_specs=[row_spec, _const_spec((1, D_MODEL)), _const_spec((D_MODEL, N_SEG * D_MODEL)),
                  _const_spec((1, D_MODEL)), _const_spec((1, HEAD_DIM)), _const_spec((1, HEAD_DIM)),
                  _const_spec((GMLP_GROUPS, CHUNK, CHUNK)), _const_spec((CHUNK, GMLP_GROUPS))],
        out_specs=[row_spec, row_spec, row_spec, row_spec, row_spec,
                   pl.BlockSpec((tm // MOBA_BLOCK, N_HEADS, HEAD_DIM, MOBA_BLOCK), lambda i: (i, 0, 0, 0)),
                   pl.BlockSpec((1, tm // MOBA_BLOCK, D_MODEL), lambda i: (i, 0, 0)),
                   row_spec, row_spec],
        out_shape=[bf_out, f32_out, f32_out, f32_out, bf_out,
                   jax.ShapeDtypeStruct((n // MOBA_BLOCK, N_HEADS, HEAD_DIM, MOBA_BLOCK), BF16),
                   jax.ShapeDtypeStruct((nt, tm // MOBA_BLOCK, D_MODEL), F32),
                   f32_out, f32_out],
        compiler_params=_params("parallel"),
        name="prompt_in",
    )(x, n1, w_in, vn, qn, kn, ws, bst)


def _prompt_attn_kernel(q_ref, kb_ref, vt_ref, ksum_ref, o_ref,
                        qt_scr, sel_scr, m_scr, l_scr, acc_scr, p_scr, s_even, s_odd):
    i = pl.program_id(2)
    nb = ksum_ref.shape[1]
    blk = MOBA_BLOCK
    c = HEAD_DIM ** -0.5 * LOG2_E
    neg = jnp.float32(-jnp.inf)
    heads = range(ATTN_HEADS)
    cols = [slice(g * HEAD_DIM, (g + 1) * HEAD_DIM) for g in heads]
    pos = lax.broadcasted_iota(jnp.int32, (nb, blk), 0).astype(F32)
    n_past = i.astype(F32)

    def scores(j, dst):
        rows = pl.ds(pl.multiple_of(j * blk, blk), blk)
        for g in heads:
            dst[g] = jnp.dot(kb_ref[rows, cols[g]], qt_scr[g], preferred_element_type=F32)

    def pv(j, alphas):
        for g in heads:
            upd = jnp.dot(vt_ref[j, g], p_scr[g], preferred_element_type=F32)
            acc_scr[g] = upd if alphas is None else alphas[g] * acc_scr[g] + upd

    for g in heads:
        qt = q_ref[:, cols[g]].T
        qt_scr[g] = (qt * c).astype(BF16)
        kbar = ksum_ref[0, :, cols[g]] * (1.0 / blk)
        gate = jnp.dot(kbar, qt, precision=lax.Precision.HIGHEST, preferred_element_type=F32)
        sel = jnp.zeros((nb, blk), F32)
        for first in _top3_select(gate, pos, n_past):
            sel = jnp.where((pos == first) & (pos < n_past), 1.0, sel)
        sel_scr[g] = sel

    scores(i, s_odd)
    scores(0, s_even)
    key = lax.broadcasted_iota(jnp.int32, (blk, blk), 0)
    qry = lax.broadcasted_iota(jnp.int32, (blk, blk), 1)
    for g in heads:
        s = jnp.where(key <= qry, s_odd[g], neg)
        m = jnp.max(s, axis=0, keepdims=True)
        p = jnp.exp2(s - m)
        m_scr[g] = m
        l_scr[g] = jnp.sum(p, axis=0, keepdims=True)
        p_scr[g] = p.astype(BF16)
    pv(i, None)

    def past_block(j, src, dst):
        scores(jnp.minimum(j + 1, nb - 1), dst)
        alphas = []
        for g in heads:
            s = src[g]
            chosen = sel_scr[g, pl.ds(j, 1), :] > 0.5
            m_old = m_scr[g]
            m_new = jnp.where(chosen, jnp.maximum(m_old, jnp.max(s, axis=0, keepdims=True)), m_old)
            p = jnp.exp2(s - jnp.where(chosen, m_new, jnp.float32(jnp.inf)))
            alpha = jnp.exp2(m_old - m_new)
            m_scr[g] = m_new
            l_scr[g] = alpha * l_scr[g] + jnp.sum(p, axis=0, keepdims=True)
            p_scr[g] = p.astype(BF16)
            alphas.append(alpha)
        pv(j, alphas)

    def two_blocks(t, _):
        past_block(2 * t, s_even, s_odd)
        past_block(2 * t + 1, s_odd, s_even)
        return 0

    lax.fori_loop(0, (i + 1) // 2, two_blocks, 0)
    for g in heads:
        o_ref[:, cols[g]] = (acc_scr[g] / l_scr[g]).T.astype(o_ref.dtype)


def _prompt_attn(q, kb, vt, ksum, batch, seq):
    n = q.shape[0]
    nb = seq // MOBA_BLOCK
    gw = ATTN_HEADS * HEAD_DIM
    ksum = ksum.reshape(batch, nb, D_MODEL)
    return pl.pallas_call(
        _prompt_attn_kernel,
        grid=(batch, N_HEADS // ATTN_HEADS, nb),
        in_specs=[pl.BlockSpec((MOBA_BLOCK, gw), lambda b, h, i: (b * nb + i, h)),
                  pl.BlockSpec((seq, gw), lambda b, h, i: (b, h)),
                  pl.BlockSpec((nb, ATTN_HEADS, HEAD_DIM, MOBA_BLOCK), lambda b, h, i: (b, h, 0, 0)),
                  pl.BlockSpec((1, nb, gw), lambda b, h, i: (b, 0, h))],
        out_specs=pl.BlockSpec((MOBA_BLOCK, gw), lambda b, h, i: (b * nb + i, h)),
        out_shape=jax.ShapeDtypeStruct((n, D_MODEL), BF16),
        scratch_shapes=[pltpu.VMEM((ATTN_HEADS, HEAD_DIM, MOBA_BLOCK), BF16),
                        pltpu.VMEM((ATTN_HEADS, nb, MOBA_BLOCK), F32),
                        pltpu.VMEM((ATTN_HEADS, 1, MOBA_BLOCK), F32),
                        pltpu.VMEM((ATTN_HEADS, 1, MOBA_BLOCK), F32),
                        pltpu.VMEM((ATTN_HEADS, HEAD_DIM, MOBA_BLOCK), F32),
                        pltpu.VMEM((ATTN_HEADS, MOBA_BLOCK, MOBA_BLOCK), BF16),
                        pltpu.VMEM((ATTN_HEADS, MOBA_BLOCK, MOBA_BLOCK), F32),
                        pltpu.VMEM((ATTN_HEADS, MOBA_BLOCK, MOBA_BLOCK), F32)],
        compiler_params=_params("parallel", "parallel", "arbitrary"),
        name="prompt_attn",
    )(q, kb, vt, ksum)


def _merge_kernel(x_ref, ya_ref, yb_ref, sga_ref, sgb_ref, wba_ref, wbb_ref, wout_ref, o_ref):
    a = jnp.dot(ya_ref[...], wba_ref[...], preferred_element_type=F32)
    b = jnp.dot(yb_ref[...], wbb_ref[...], preferred_element_type=F32)
    mix = (sga_ref[...] * a + sgb_ref[...] * b).astype(BF16)
    o_ref[...] = x_ref[...] + jnp.dot(mix, wout_ref[...], preferred_element_type=F32)


def _merge(x, ya, yb, sga, sgb, wba, wbb, wout, tm):
    n = x.shape[0]
    row_spec = pl.BlockSpec((tm, D_MODEL), lambda i: (i, 0))
    w_spec = _const_spec((D_MODEL, D_MODEL))
    return pl.pallas_call(
        _merge_kernel,
        grid=(n // tm,),
        in_specs=[row_spec] * 5 + [w_spec] * 3,
        out_specs=row_spec,
        out_shape=jax.ShapeDtypeStruct((n, D_MODEL), F32),
        compiler_params=_params("parallel"),
        name="merge",
    )(x, ya, yb, sga, sgb, wba, wbb, wout)


def _prompt_ffn_kernel(x_ref, n2_ref, wu_ref, cw_ref, cb_ref, wd_ref,
                       o_ref, st_ref, h_scr, carry_scr, *, tiles_per_seq):
    tm = x_ref.shape[0]

    @pl.when(pl.program_id(0) % tiles_per_seq == 0)
    def _():
        carry_scr[...] = jnp.zeros_like(carry_scr)

    x = x_ref[...]
    h_scr[...] = _rms(x, n2_ref[...]).astype(BF16)
    row = lax.broadcasted_iota(jnp.int32, (8, FF_CHUNK), 0)

    def conv(part, c):
        up = jnp.dot(h_scr[...], wu_ref[part, c], preferred_element_type=F32)
        prev = carry_scr[part, c]
        r1, r2 = pltpu.roll(up, 1, 0), pltpu.roll(up, 2, 0)
        top1 = jnp.where(row == 0, prev[1:2, :], r1[0:8, :])
        top2 = jnp.where(row == 0, prev[0:1, :], jnp.where(row == 1, prev[1:2, :], r2[0:8, :]))
        u1 = jnp.concatenate([top1, r1[8:, :]], axis=0)
        u2 = jnp.concatenate([top2, r2[8:, :]], axis=0)
        carry_scr[part, c] = up[tm - 2:tm, :]
        cw = cw_ref[part, c]
        return cb_ref[part, c] + cw[0:1, :] * u2 + cw[1:2, :] * u1 + cw[2:3, :] * up

    acc = x
    for c in range(N_FF_CHUNKS):
        act = (jax.nn.gelu(conv(0, c)) * conv(1, c)).astype(BF16)
        acc = acc + jnp.dot(act, wd_ref[c], preferred_element_type=F32)
    o_ref[...] = acc
    st_ref[0] = carry_scr[...]


def _prompt_ffn(x, n2, wu_c, cw_c, cb_c, wd_c, batch, seq):
    n = x.shape[0]
    tm = ROW_TILE
    tiles_per_seq = seq // tm
    row_spec = pl.BlockSpec((tm, D_MODEL), lambda i: (i, 0))
    st_shape = (2, N_FF_CHUNKS, CONV_W - 1, FF_CHUNK)
    return pl.pallas_call(
        functools.partial(_prompt_ffn_kernel, tiles_per_seq=tiles_per_seq),
        grid=(n // tm,),
        in_specs=[row_spec, _const_spec((1, D_MODEL)),
                  _const_spec((2, N_FF_CHUNKS, D_MODEL, FF_CHUNK)),
                  _const_spec((2, N_FF_CHUNKS, CONV_W, FF_CHUNK)),
                  _const_spec((2, N_FF_CHUNKS, 1, FF_CHUNK)),
                  _const_spec((N_FF_CHUNKS, FF_CHUNK, D_MODEL))],
        out_specs=[row_spec,
                   pl.BlockSpec((1,) + st_shape, lambda i: (i // tiles_per_seq, 0, 0, 0, 0))],
        out_shape=[jax.ShapeDtypeStruct((n, D_MODEL), F32),
                   jax.ShapeDtypeStruct((batch,) + st_shape, F32)],
        scratch_shapes=[pltpu.VMEM((tm, D_MODEL), BF16), pltpu.VMEM(st_shape, F32)],
        compiler_params=_params("arbitrary"),
        name="prompt_ffn",
    )(x, n2, wu_c, cw_c, cb_c, wd_c)


def _sample_in_kernel(x_ref, n1_ref, w_ref, vn_ref, qn_ref, kn_ref, wd0_ref, b0_ref,
                      ya_ref, q_ref, k_ref, vv_ref, gv_ref, sga_ref, sgb_ref):
    h = _rms(x_ref[...], n1_ref[...]).astype(BF16)

    def seg(s):
        return jnp.dot(h, w_ref[:, s * D_MODEL:(s + 1) * D_MODEL], preferred_element_type=F32)

    u = jax.nn.gelu(seg(0))
    v = _rms(jax.nn.gelu(seg(1)), vn_ref[...])
    gv_ref[...] = v
    ya_ref[...] = (u * (wd0_ref[...] * v + b0_ref[...])).astype(BF16)
    aq = seg(2)
    ak = seg(3)
    for hd in range(N_HEADS):
        cs = slice(hd * HEAD_DIM, (hd + 1) * HEAD_DIM)
        q_ref[:, cs] = _rms(aq[:, cs], qn_ref[...])
        k_ref[:, cs] = _rms(ak[:, cs], kn_ref[...])
    vv_ref[...] = seg(4)
    sga_ref[...] = jax.nn.sigmoid(seg(5))
    sgb_ref[...] = jax.nn.sigmoid(seg(6))


def _sample_in(x, n1, w_in, vn, qn, kn, wd0, b0):
    n = x.shape[0]
    f32_out = jax.ShapeDtypeStruct((n, D_MODEL), F32)
    return pl.pallas_call(
        _sample_in_kernel,
        out_shape=[jax.ShapeDtypeStruct((n, D_MODEL), BF16)] + [f32_out] * 6,
        compiler_params=pltpu.CompilerParams(vmem_limit_bytes=VMEM_LIMIT),
        name="sample_in",
    )(x, n1, w_in, vn, qn, kn, wd0, b0)


def _block_sum_kernel(pt_ref, *refs):
    pages, out_ref = refs[:PAGES_PER_STEP], refs[PAGES_PER_STEP]
    for r in range(PAGES_PER_STEP // PAGES_PER_BLOCK):
        s = jnp.sum(pages[PAGES_PER_BLOCK * r][0, 0], axis=0)
        for p in range(1, PAGES_PER_BLOCK):
            s = s + jnp.sum(pages[PAGES_PER_BLOCK * r + p][0, 0], axis=0)
        out_ref[0, 0, r] = s


def _block_sums(cache_k, page_table_flat, n_seq, n_pages):
    depth = cache_k.shape[0]
    steps = n_pages // PAGES_PER_STEP
    blocks_per_step = PAGES_PER_STEP // PAGES_PER_BLOCK

    def page_spec(r):
        return pl.BlockSpec((1, 1, PAGE_SIZE, N_HEADS, HEAD_DIM),
                            lambda l, b, g, pt: (l, pt[b * n_pages + g * PAGES_PER_STEP + r], 0, 0, 0))

    return pl.pallas_call(
        _block_sum_kernel,
        grid_spec=pltpu.PrefetchScalarGridSpec(
            num_scalar_prefetch=1,
            grid=(depth, n_seq, steps),
            in_specs=[page_spec(r) for r in range(PAGES_PER_STEP)],
            out_specs=pl.BlockSpec((1, 1, blocks_per_step, N_HEADS, HEAD_DIM),
                                   lambda l, b, g, pt: (l, b, g, 0, 0)),
        ),
        out_shape=jax.ShapeDtypeStruct((depth, n_seq, steps * blocks_per_step, N_HEADS, HEAD_DIM), F32),
        compiler_params=_params("parallel", "parallel", "arbitrary"),
        name="cache_block_sums",
    )(page_table_flat, *([cache_k] * PAGES_PER_STEP))


def _sample_gate_kernel(q_ref, k_ref, bsum_ref, idx_ref, *, n_past):
    q = q_ref[0]
    inv = 1.0 / MOBA_BLOCK
    past = (bsum_ref[0, 0] * inv) * q
    own = (k_ref[0] * inv) * q
    prod = jnp.concatenate([past, own[None]], axis=0)
    ones = jnp.ones((HEAD_DIM, HEAD_DIM), F32)
    gate = jnp.dot(prod.reshape((n_past + 1) * N_HEADS, HEAD_DIM), ones,
                   precision=lax.Precision.HIGHEST, preferred_element_type=F32)
    gate = gate.reshape(n_past + 1, N_HEADS, HEAD_DIM)
    pos = lax.broadcasted_iota(jnp.int32, gate.shape, 0).astype(F32)
    lane = lax.broadcasted_iota(jnp.int32, (N_HEADS, HEAD_DIM), 1)
    out = jnp.zeros((N_HEADS, HEAD_DIM), jnp.int32)
    for r, first in enumerate(_top3_select(gate, pos, jnp.float32(n_past))):
        out = jnp.where(lane == r, first[0].astype(jnp.int32), out)
    idx_ref[0] = out


def _sample_gate(q, k, bsum, layer):
    n = q.shape[0]
    n_past = bsum.shape[2]
    vec_spec = pl.BlockSpec((1, N_HEADS, HEAD_DIM), lambda b: (b, 0, 0))
    return pl.pallas_call(
        functools.partial(_sample_gate_kernel, n_past=n_past),
        grid=(n,),
        in_specs=[vec_spec, vec_spec,
                  pl.BlockSpec((1, 1, n_past, N_HEADS, HEAD_DIM), lambda b: (layer, b, 0, 0, 0))],
        out_specs=vec_spec,
        out_shape=jax.ShapeDtypeStruct((n, N_HEADS, HEAD_DIM), jnp.int32),
        compiler_params=_params("parallel"),
        name="sample_gate",
    )(q, k, bsum)


def _sample_attn_kernel(pt_ref, idx_ref, q_ref, kn_ref, vn_ref, ck_ref, cv_ref, o_ref,
                        kbuf, vbuf, sem, *, layer, n_pages):
    b = pl.program_id(0)
    n_seq = pl.num_programs(0)

    def page_copies(seq, slot):
        out = []
        for h in range(N_HEADS):
            for j in range(N_SEL_PAGES):
                r, p = divmod(j, PAGES_PER_BLOCK)
                blk = idx_ref[(seq * N_HEADS + h) * MOBA_TOPK + r]
                phys = pt_ref[seq * n_pages + blk * PAGES_PER_BLOCK + p]
                rows = pl.ds(j * PAGE_SIZE, PAGE_SIZE)
                out.append(pltpu.make_async_copy(ck_ref.at[layer, phys, :, h, :],
                                                 kbuf.at[slot, h, rows, :], sem.at[slot]))
                out.append(pltpu.make_async_copy(cv_ref.at[layer, phys, :, h, :],
                                                 vbuf.at[slot, h, rows, :], sem.at[slot]))
        return out

    @pl.when(b == 0)
    def _():
        for cp in page_copies(0, 0):
            cp.start()

    @pl.when(b + 1 < n_seq)
    def _():
        for cp in page_copies(b + 1, (b + 1) % 2):
            cp.start()

    slot = b % 2
    for cp in page_copies(b, slot):
        cp.wait()

    scale = HEAD_DIM ** -0.5
    for h in range(N_HEADS):
        q = q_ref[0, h:h + 1, :]
        s = jnp.sum(kbuf[slot, h] * q, axis=-1, keepdims=True) * scale
        s_own = jnp.sum(kn_ref[0, h:h + 1, :] * q, axis=-1, keepdims=True) * scale
        m = jnp.maximum(jnp.max(s, axis=0, keepdims=True), s_own)
        p = jnp.exp(s - m)
        p_own = jnp.exp(s_own - m)
        l = jnp.sum(p, axis=0, keepdims=True) + p_own
        acc = jnp.sum(p * vbuf[slot, h], axis=0, keepdims=True) + p_own * vn_ref[0, h:h + 1, :]
        o_ref[0, h:h + 1, :] = (acc / l).astype(o_ref.dtype)


def _sample_attn(q, k_new, v_new, cache_k, cache_v, page_table_flat, idx_flat, layer, n_pages):
    n = q.shape[0]
    vec_spec = pl.BlockSpec((1, N_HEADS, HEAD_DIM), lambda b, pt, ix: (b, 0, 0))
    any_spec = pl.BlockSpec(memory_space=pl.ANY)
    buf = pltpu.VMEM((2, N_HEADS, N_SEL_PAGES * PAGE_SIZE, HEAD_DIM), F32)
    return pl.pallas_call(
        functools.partial(_sample_attn_kernel, layer=layer, n_pages=n_pages),
        grid_spec=pltpu.PrefetchScalarGridSpec(
            num_scalar_prefetch=2,
            grid=(n,),
            in_specs=[vec_spec, vec_spec, vec_spec, any_spec, any_spec],
            out_specs=vec_spec,
            scratch_shapes=[buf, buf, pltpu.SemaphoreType.DMA((2,))],
        ),
        out_shape=jax.ShapeDtypeStruct((n, N_HEADS, HEAD_DIM), BF16),
        compiler_params=_params("arbitrary"),
        name="sample_attn",
    )(page_table_flat, idx_flat, q, k_new, v_new, cache_k, cache_v)


def _sample_out_kernel(x_ref, ya_ref, yb_ref, sga_ref, sgb_ref, wba_ref, wbb_ref, wout_ref,
                       n2_ref, wu_ref, cw_ref, cb_ref, wd_ref, h0_ref, h1_ref, o_ref, up_ref):
    a = jnp.dot(ya_ref[...], wba_ref[...], preferred_element_type=F32)
    b = jnp.dot(yb_ref[...], wbb_ref[...], preferred_element_type=F32)
    mix = (sga_ref[...] * a + sgb_ref[...] * b).astype(BF16)
    x = x_ref[...] + jnp.dot(mix, wout_ref[...], preferred_element_type=F32)
    h = _rms(x, n2_ref[...]).astype(BF16)
    up = jnp.dot(h, wu_ref[...], preferred_element_type=F32)
    up_ref[...] = up
    c = cb_ref[...] + cw_ref[0:1, :] * h0_ref[...] + cw_ref[1:2, :] * h1_ref[...] + cw_ref[2:3, :] * up
    act = (jax.nn.gelu(c[:, :D_FF]) * c[:, D_FF:]).astype(BF16)
    o_ref[...] = x + jnp.dot(act, wd_ref[...], preferred_element_type=F32)


def _sample_out(x, ya, yb, sga, sgb, wba, wbb, wout, n2, wu, cw, cb, wd, h0, h1):
    n = x.shape[0]
    return pl.pallas_call(
        _sample_out_kernel,
        out_shape=[jax.ShapeDtypeStruct((n, D_MODEL), F32), jax.ShapeDtypeStruct((n, 2 * D_FF), F32)],
        compiler_params=pltpu.CompilerParams(vmem_limit_bytes=VMEM_LIMIT),
        name="sample_out",
    )(x, ya, yb, sga, sgb, wba, wbb, wout, n2, wu, cw, cb, wd, h0, h1)


def _chunk_cols(a):
    lead = a.shape[:-1]
    a = a.reshape(lead + (2, N_FF_CHUNKS, FF_CHUNK))
    return jnp.moveaxis(a, (-3, -2), (0, 1))


def kernel(x_prompt, x_sample, cache_k, cache_v, state_conv, page_table, norm1_w, w_in, gmlp_vnorm_w, gmlp_ws, gmlp_bs, q_norm_w, k_norm_w, w_branch_a, w_branch_b, w_out, norm2_w, w_up, conv_w, conv_b, w_down):
    batch, seq, _ = x_prompt.shape
    n_seq, dec_seq, _ = x_sample.shape
    depth = w_in.shape[0]
    n_pages = page_table.shape[1]
    assert dec_seq == 1 and seq % ROW_TILE == 0 and ROW_TILE % MOBA_BLOCK == 0
    assert n_pages % PAGES_PER_STEP == 0 and (n_pages * PAGE_SIZE) % MOBA_BLOCK == 0

    pt_flat = page_table.reshape(-1)
    bsum = _block_sums(cache_k, pt_flat, n_seq, n_pages)
    heads = lambda a: a.reshape(n_seq, N_HEADS, HEAD_DIM)

    xp = x_prompt.reshape(batch * seq, D_MODEL)
    xs = x_sample.reshape(n_seq, D_MODEL)
    row = lambda a: a.reshape(1, -1)
    k_p, v_p, k_s, v_s, gv_s, cv_p, cv_s = [], [], [], [], [], [], []
    for l in range(depth):
        w_in_b = w_in[l].astype(BF16)
        wba, wbb, wout = w_branch_a[l].astype(BF16), w_branch_b[l].astype(BF16), w_out[l].astype(BF16)
        wu_b, wd_b = w_up[l].astype(BF16), w_down[l].astype(BF16)
        n1, n2, vn, qn, kn = row(norm1_w[l]), row(norm2_w[l]), row(gmlp_vnorm_w[l]), row(q_norm_w[l]), row(k_norm_w[l])

        ya, q, k, vv, kb, vt, ksum, sga, sgb = _prompt_in(xp, n1, w_in_b, vn, qn, kn, gmlp_ws[l], gmlp_bs[l].T)
        yb = _prompt_attn(q, kb, vt, ksum, batch, seq)
        xp = _merge(xp, ya, yb, sga, sgb, wba, wbb, wout, ROW_TILE)
        xp, st = _prompt_ffn(xp, n2, _chunk_cols(wu_b), _chunk_cols(conv_w[l]),
                             _chunk_cols(conv_b[l].reshape(1, -1)),
                             wd_b.reshape(N_FF_CHUNKS, FF_CHUNK, D_MODEL), batch, seq)
        k_p.append(k.reshape(batch, seq, N_HEADS, HEAD_DIM))
        v_p.append(vv.reshape(batch, seq, N_HEADS, HEAD_DIM))
        cv_p.append(jnp.transpose(st, (0, 3, 1, 2, 4)).reshape(batch, CONV_W - 1, 2 * D_FF))

        wd0 = jnp.repeat(gmlp_ws[l][:, 0, 0], GROUP_DIM).reshape(1, -1)
        b0 = jnp.repeat(gmlp_bs[l][:, 0], GROUP_DIM).reshape(1, -1)
        ya, q, k, vv, gv, sga, sgb = _sample_in(xs, n1, w_in_b, vn, qn, kn, wd0, b0)
        idx = _sample_gate(heads(q), heads(k), bsum, l)
        idx_flat = idx[:, :, :MOBA_TOPK].reshape(-1)
        yb = _sample_attn(heads(q), heads(k), heads(vv), cache_k, cache_v, pt_flat, idx_flat, l, n_pages)
        yb = yb.reshape(n_seq, D_MODEL)
        hist = state_conv[l]
        xs, up = _sample_out(xs, ya, yb, sga, sgb, wba, wbb, wout, n2, wu_b, conv_w[l],
                             conv_b[l].reshape(1, -1), wd_b, hist[:, 0, :], hist[:, 1, :])
        k_s.append(k.reshape(n_seq, 1, N_HEADS, HEAD_DIM))
        v_s.append(vv.reshape(n_seq, 1, N_HEADS, HEAD_DIM))
        gv_s.append(gv.reshape(n_seq, 1, D_MODEL))
        cv_s.append(jnp.stack([hist[:, 1, :], up], axis=1))

    return (xp.reshape(batch, seq, D_MODEL), xs.reshape(n_seq, 1, D_MODEL),
            jnp.stack(k_p), jnp.stack(v_p), jnp.stack(k_s), jnp.stack(v_s),
            jnp.stack(gv_s), jnp.stack(cv_p), jnp.stack(cv_s))
```

```python
import functools

import jax
import jax.numpy as jnp
from jax import lax
from jax.experimental import pallas as pl
from jax.experimental.pallas import tpu as pltpu

F32 = jnp.float32
BF16 = jnp.bfloat16

D_MODEL = 1024
N_HEADS = 8
HEAD_DIM = 128
GMLP_GROUPS = 8
GROUP_DIM = 128
CHUNK = 128
MOBA_BLOCK = 256
MOBA_TOPK = 3
D_FF = 2816
CONV_W = 3
EPS = 1e-6
PAGE_SIZE = 128
N_SEG = 7
FF_CHUNK = 256
N_FF_CHUNKS = D_FF // FF_CHUNK
FFN_SUB_PAGES = 8
IN_SUB_PAGES = 4
IN_PAGE_SHARE = 4
SUM_CHAINS = 8
VT_ROWS = HEAD_DIM + 16
PAGES_PER_BLOCK = MOBA_BLOCK // PAGE_SIZE
N_SEL_PAGES = MOBA_TOPK * PAGES_PER_BLOCK
ROW_TILE = 512
ATTN_HEADS = 4
LOG2_E = 1.4426950408889634
VMEM_LIMIT = 60 * 1024 * 1024


def _rms(x, w):
    ms = jnp.mean(x * x, axis=-1, keepdims=True)
    return x * lax.rsqrt(ms + EPS) * w


def _const_spec(shape):
    nd = len(shape)
    return pl.BlockSpec(shape, lambda *_: (0,) * nd, pipeline_mode=pl.Buffered(1))


def _params(*sem):
    return pltpu.CompilerParams(dimension_semantics=sem, vmem_limit_bytes=VMEM_LIMIT)


def _top3_select(gate, pos, n_valid):
    neg = jnp.float32(-jnp.inf)
    gate = jnp.where(pos < n_valid, gate, neg)
    picks = []
    for _ in range(MOBA_TOPK):
        m = jnp.max(gate, axis=0, keepdims=True)
        first = jnp.min(jnp.where(gate == m, pos, jnp.float32(2 ** 30)), axis=0, keepdims=True)
        picks.append(first)
        gate = jnp.where(pos == first, neg, gate)
    return picks


class _PageSums:
    def __init__(self, pt_ref, cache_ref, out_ref, buf, sem, *, layer, first_page, pages_per_step):
        self.pt_ref, self.cache_ref, self.out_ref, self.buf, self.sem = pt_ref, cache_ref, out_ref, buf, sem
        self.layer, self.first_page = layer, first_page
        self.sub = buf.shape[1]
        self.n_sub = pages_per_step // self.sub
        assert pages_per_step % self.sub == 0 and self.n_sub % 2 == 0 and self.sub % PAGES_PER_BLOCK == 0
        self.step = pl.program_id(0)
        self.n_steps = pl.num_programs(0)

    def _copies(self, step, s):
        slot = s % 2
        base = self.first_page + (step * self.n_sub + s) * self.sub
        return [pltpu.make_async_copy(self.cache_ref.at[self.layer, self.pt_ref[base + p]],
                                      self.buf.at[slot, p], self.sem.at[slot]) for p in range(self.sub)]

    def _start(self, step, s):
        for cp in self._copies(step, s):
            cp.start()

    def prime(self):
        @pl.when(self.step == 0)
        def _():
            self._start(0, 0)
            self._start(0, 1)

    def _process(self, s):
        for cp in self._copies(self.step, s):
            cp.wait()
        slot = s % 2
        per_sub = self.sub // PAGES_PER_BLOCK
        lanes_of = (PAGE_SIZE // SUM_CHAINS, SUM_CHAINS, N_HEADS, HEAD_DIM)
        for r in range(per_sub):
            part = jnp.sum(self.buf[slot, PAGES_PER_BLOCK * r].reshape(lanes_of), axis=0)
            for p in range(1, PAGES_PER_BLOCK):
                part = part + jnp.sum(self.buf[slot, PAGES_PER_BLOCK * r + p].reshape(lanes_of), axis=0)
            self.out_ref[s * per_sub + r] = jnp.sum(part, axis=0)
        if s + 2 < self.n_sub:
            self._start(self.step, s + 2)
        else:
            @pl.when(self.step + 1 < self.n_steps)
            def _():
                self._start(self.step + 1, s + 2 - self.n_sub)

    def at_point(self, k, n_points):
        for s in range(self.n_sub):
            if s * n_points // self.n_sub == k:
                self._process(s)


def _prompt_in_kernel(pt_ref, x_ref, n1_ref, w_ref, vn_ref, qn_ref, kn_ref, ws_ref, bst_ref, cache_ref,
                      ya_ref, q_ref, k_ref, vv_ref, kb_ref, vt_ref, ksum_ref, sga_ref, sgb_ref, bsum_ref,
                      page_buf, page_sem, *, layer, first_page, pages_per_step):
    tm = x_ref.shape[0]
    pages = _PageSums(pt_ref, cache_ref, bsum_ref, page_buf, page_sem, layer=layer,
                      first_page=first_page, pages_per_step=pages_per_step)
    pages.prime()
    h = _rms(x_ref[...], n1_ref[...]).astype(BF16)

    def seg(s):
        pages.at_point(s, N_SEG)
        return jnp.dot(h, w_ref[:, s * D_MODEL:(s + 1) * D_MODEL], preferred_element_type=F32)

    u = jax.nn.gelu(seg(0))
    v = _rms(jax.nn.gelu(seg(1)), vn_ref[...]).astype(BF16)
    row = lax.broadcasted_iota(jnp.int32, (CHUNK, CHUNK), 0)
    col = lax.broadcasted_iota(jnp.int32, (CHUNK, CHUNK), 1)
    for g in range(GMLP_GROUPS):
        wg = jnp.where(row >= col, ws_ref[g], 0.0).astype(BF16)
        bg = bst_ref[:, g:g + 1]
        cs = slice(g * GROUP_DIM, (g + 1) * GROUP_DIM)
        for c in range(tm // CHUNK):
            rs = slice(c * CHUNK, (c + 1) * CHUNK)
            mixed = jnp.dot(wg, v[rs, cs], preferred_element_type=F32) + bg
            ya_ref[rs, cs] = (u[rs, cs] * mixed).astype(BF16)

    aq = seg(2)
    for hd in range(N_HEADS):
        cs = slice(hd * HEAD_DIM, (hd + 1) * HEAD_DIM)
        q_ref[:, cs] = _rms(aq[:, cs], qn_ref[...])
    ak = seg(3)
    for hd in range(N_HEADS):
        cs = slice(hd * HEAD_DIM, (hd + 1) * HEAD_DIM)
        kh = _rms(ak[:, cs], kn_ref[...])
        k_ref[:, cs] = kh
        kb_ref[:, cs] = kh.astype(BF16)
    for blk in range(tm // MOBA_BLOCK):
        rs = slice(blk * MOBA_BLOCK, (blk + 1) * MOBA_BLOCK)
        ksum_ref[0, blk:blk + 1, :] = jnp.sum(k_ref[rs, :], axis=0, keepdims=True)
    av = seg(4)
    vv_ref[...] = av
    ones = jnp.ones((VT_ROWS - HEAD_DIM, MOBA_BLOCK), BF16)
    for blk in range(tm // MOBA_BLOCK):
        rs = slice(blk * MOBA_BLOCK, (blk + 1) * MOBA_BLOCK)
        for hd in range(N_HEADS):
            vt_ref[blk, hd, 0:HEAD_DIM, :] = av[rs, hd * HEAD_DIM:(hd + 1) * HEAD_DIM].T.astype(BF16)
            vt_ref[blk, hd, HEAD_DIM:VT_ROWS, :] = ones
    sga_ref[...] = jax.nn.sigmoid(seg(5))
    sgb_ref[...] = jax.nn.sigmoid(seg(6))


def _prompt_in(x, n1, w_in, vn, qn, kn, ws, bst, page_table_flat, cache_k, layer, first_page, pages_per_step):
    n = x.shape[0]
    tm = ROW_TILE
    nt = n // tm
    row_spec = pl.BlockSpec((tm, D_MODEL), lambda i, pt: (i, 0))
    f32_out = jax.ShapeDtypeStruct((n, D_MODEL), F32)
    bf_out = jax.ShapeDtypeStruct((n, D_MODEL), BF16)
    bsum_spec, bsum_shape, page_scratch = _page_sum_specs(nt, pages_per_step, IN_SUB_PAGES)
    return pl.pallas_call(
        functools.partial(_prompt_in_kernel, layer=layer, first_page=first_page, pages_per_step=pages_per_step),
        grid_spec=pltpu.PrefetchScalarGridSpec(
            num_scalar_prefetch=1,
            grid=(nt,),
            in_specs=[row_spec, _const_spec((1, D_MODEL)), _const_spec((D_MODEL, N_SEG * D_MODEL)),
                      _const_spec((1, D_MODEL)), _const_spec((1, HEAD_DIM)), _const_spec((1, HEAD_DIM)),
                      _const_spec((GMLP_GROUPS, CHUNK, CHUNK)), _const_spec((CHUNK, GMLP_GROUPS)),
                      pl.BlockSpec(memory_space=pl.ANY)],
            out_specs=[row_spec, row_spec, row_spec, row_spec, row_spec,
                       pl.BlockSpec((tm // MOBA_BLOCK, N_HEADS, VT_ROWS, MOBA_BLOCK), lambda i, pt: (i, 0, 0, 0)),
                       pl.BlockSpec((1, tm // MOBA_BLOCK, D_MODEL), lambda i, pt: (i, 0, 0)),
                       row_spec, row_spec, bsum_spec],
            scratch_shapes=page_scratch,
        ),
        out_shape=[bf_out, f32_out, f32_out, f32_out, bf_out,
                   jax.ShapeDtypeStruct((n // MOBA_BLOCK, N_HEADS, VT_ROWS, MOBA_BLOCK), BF16),
                   jax.ShapeDtypeStruct((nt, tm // MOBA_BLOCK, D_MODEL), F32),
                   f32_out, f32_out, bsum_shape],
        compiler_params=_params("arbitrary"),
        name="prompt_in",
    )(page_table_flat, x, n1, w_in, vn, qn, kn, ws, bst, cache_k)


def _prompt_attn_kernel(q_ref, kb_ref, vt_ref, ksum_ref, o_ref,
                        qt_scr, sel_scr, m_scr, acc_scr, p_scr, s_even, s_odd):
    i = pl.program_id(2)
    nb = ksum_ref.shape[1]
    blk = MOBA_BLOCK
    c = HEAD_DIM ** -0.5 * LOG2_E
    neg = jnp.float32(-jnp.inf)
    heads = range(ATTN_HEADS)
    cols = [slice(g * HEAD_DIM, (g + 1) * HEAD_DIM) for g in heads]
    pos = lax.broadcasted_iota(jnp.int32, (nb, blk), 0).astype(F32)
    n_past = i.astype(F32)

    def scores(j, dst):
        rows = pl.ds(pl.multiple_of(j * blk, blk), blk)
        for g in heads:
            dst[g] = jnp.dot(kb_ref[rows, cols[g]], qt_scr[g], preferred_element_type=F32)

    def pv(j, alphas):
        for g in heads:
            upd = jnp.dot(vt_ref[j, g], p_scr[g], preferred_element_type=F32)
            acc_scr[g] = upd if alphas is None else alphas[g] * acc_scr[g] + upd

    for g in heads:
        qt = q_ref[:, cols[g]].T
        qt_scr[g] = (qt * c).astype(BF16)
        kbar = ksum_ref[0, :, cols[g]] * (1.0 / blk)
        gate = jnp.dot(kbar, qt, precision=lax.Precision.HIGHEST, preferred_element_type=F32)
        sel = jnp.zeros((nb, blk), F32)
        for first in _top3_select(gate, pos, n_past):
            sel = jnp.where((pos == first) & (pos < n_past), 1.0, sel)
        sel_scr[g] = sel

    scores(i, s_odd)
    scores(0, s_even)
    key = lax.broadcasted_iota(jnp.int32, (blk, blk), 0)
    qry = lax.broadcasted_iota(jnp.int32, (blk, blk), 1)
    for g in heads:
        s = jnp.where(key <= qry, s_odd[g], neg)
        m = jnp.max(s, axis=0, keepdims=True)
        p = jnp.exp2(s - m)
        m_scr[g] = m
        p_scr[g] = p.astype(BF16)
    pv(i, None)

    def past_block(j, src, dst):
        scores(jnp.minimum(j + 1, nb - 1), dst)
        alphas = []
        for g in heads:
            s = src[g]
            chosen = sel_scr[g, pl.ds(j, 1), :] > 0.5
            m_old = m_scr[g]
            m_new = jnp.where(chosen, jnp.maximum(m_old, jnp.max(s, axis=0, keepdims=True)), m_old)
            p = jnp.exp2(s - jnp.where(chosen, m_new, jnp.float32(jnp.inf)))
            alpha = jnp.exp2(m_old - m_new)
            m_scr[g] = m_new
            p_scr[g] = p.astype(BF16)
            alphas.append(alpha)
        pv(j, alphas)

    def two_blocks(t, _):
        past_block(2 * t, s_even, s_odd)
        past_block(2 * t + 1, s_odd, s_even)
        return 0

    lax.fori_loop(0, (i + 1) // 2, two_blocks, 0)
    for g in heads:
        out = acc_scr[g, 0:HEAD_DIM, :] / acc_scr[g, HEAD_DIM:HEAD_DIM + 1, :]
        o_ref[:, cols[g]] = out.T.astype(o_ref.dtype)


def _prompt_attn(q, kb, vt, ksum, batch, seq):
    n = q.shape[0]
    nb = seq // MOBA_BLOCK
    gw = ATTN_HEADS * HEAD_DIM
    ksum = ksum.reshape(batch, nb, D_MODEL)
    return pl.pallas_call(
        _prompt_attn_kernel,
        grid=(batch, N_HEADS // ATTN_HEADS, nb),
        in_specs=[pl.BlockSpec((MOBA_BLOCK, gw), lambda b, h, i: (b * nb + i, h)),
                  pl.BlockSpec((seq, gw), lambda b, h, i: (b, h)),
                  pl.BlockSpec((nb, ATTN_HEADS, VT_ROWS, MOBA_BLOCK), lambda b, h, i: (b, h, 0, 0)),
                  pl.BlockSpec((1, nb, gw), lambda b, h, i: (b, 0, h))],
        out_specs=pl.BlockSpec((MOBA_BLOCK, gw), lambda b, h, i: (b * nb + i, h)),
        out_shape=jax.ShapeDtypeStruct((n, D_MODEL), BF16),
        scratch_shapes=[pltpu.VMEM((ATTN_HEADS, HEAD_DIM, MOBA_BLOCK), BF16),
                        pltpu.VMEM((ATTN_HEADS, nb, MOBA_BLOCK), F32),
                        pltpu.VMEM((ATTN_HEADS, 1, MOBA_BLOCK), F32),
                        pltpu.VMEM((ATTN_HEADS, VT_ROWS, MOBA_BLOCK), F32),
                        pltpu.VMEM((ATTN_HEADS, MOBA_BLOCK, MOBA_BLOCK), BF16),
                        pltpu.VMEM((ATTN_HEADS, MOBA_BLOCK, MOBA_BLOCK), F32),
                        pltpu.VMEM((ATTN_HEADS, MOBA_BLOCK, MOBA_BLOCK), F32)],
        compiler_params=_params("parallel", "parallel", "arbitrary"),
        name="prompt_attn",
    )(q, kb, vt, ksum)


def _merge_kernel(x_ref, ya_ref, yb_ref, sga_ref, sgb_ref, wba_ref, wbb_ref, wout_ref, o_ref):
    a = jnp.dot(ya_ref[...], wba_ref[...], preferred_element_type=F32)
    b = jnp.dot(yb_ref[...], wbb_ref[...], preferred_element_type=F32)
    mix = (sga_ref[...] * a + sgb_ref[...] * b).astype(BF16)
    o_ref[...] = x_ref[...] + jnp.dot(mix, wout_ref[...], preferred_element_type=F32)


def _merge(x, ya, yb, sga, sgb, wba, wbb, wout, tm):
    n = x.shape[0]
    row_spec = pl.BlockSpec((tm, D_MODEL), lambda i: (i, 0))
    w_spec = _const_spec((D_MODEL, D_MODEL))
    return pl.pallas_call(
        _merge_kernel,
        grid=(n // tm,),
        in_specs=[row_spec] * 5 + [w_spec] * 3,
        out_specs=row_spec,
        out_shape=jax.ShapeDtypeStruct((n, D_MODEL), F32),
        compiler_params=_params("parallel"),
        name="merge",
    )(x, ya, yb, sga, sgb, wba, wbb, wout)


def _prompt_ffn_kernel(pt_ref, x_ref, n2_ref, wu_ref, cw_ref, cb_ref, wd_ref, cache_ref,
                       o_ref, st_ref, bsum_ref, h_scr, carry_scr, page_buf, page_sem,
                       *, tiles_per_seq, layer, first_page, pages_per_step):
    tm = x_ref.shape[0]
    pages = _PageSums(pt_ref, cache_ref, bsum_ref, page_buf, page_sem, layer=layer,
                      first_page=first_page, pages_per_step=pages_per_step)
    pages.prime()

    @pl.when(pl.program_id(0) % tiles_per_seq == 0)
    def _():
        carry_scr[...] = jnp.zeros_like(carry_scr)

    x = x_ref[...]
    h_scr[...] = _rms(x, n2_ref[...]).astype(BF16)
    row = lax.broadcasted_iota(jnp.int32, (8, FF_CHUNK), 0)

    def conv(part, c):
        up = jnp.dot(h_scr[...], wu_ref[part, c], preferred_element_type=F32)
        prev = carry_scr[part, c]
        r1, r2 = pltpu.roll(up, 1, 0), pltpu.roll(up, 2, 0)
        top1 = jnp.where(row == 0, prev[1:2, :], r1[0:8, :])
        top2 = jnp.where(row == 0, prev[0:1, :], jnp.where(row == 1, prev[1:2, :], r2[0:8, :]))
        u1 = jnp.concatenate([top1, r1[8:, :]], axis=0)
        u2 = jnp.concatenate([top2, r2[8:, :]], axis=0)
        carry_scr[part, c] = up[tm - 2:tm, :]
        cw = cw_ref[part, c]
        return cb_ref[part, c] + cw[0:1, :] * u2 + cw[1:2, :] * u1 + cw[2:3, :] * up

    acc = x
    for c in range(N_FF_CHUNKS):
        pages.at_point(c, N_FF_CHUNKS)
        act = (jax.nn.gelu(conv(0, c)) * conv(1, c)).astype(BF16)
        acc = acc + jnp.dot(act, wd_ref[c], preferred_element_type=F32)
    o_ref[...] = acc
    st_ref[0] = carry_scr[...]


def _page_sum_specs(n_steps, pages_per_step, sub_pages):
    blocks = pages_per_step // PAGES_PER_BLOCK
    out_spec = pl.BlockSpec((blocks, N_HEADS, HEAD_DIM), lambda i, pt: (i, 0, 0))
    out_shape = jax.ShapeDtypeStruct((n_steps * blocks, N_HEADS, HEAD_DIM), F32)
    scratch = [pltpu.VMEM((2, sub_pages, PAGE_SIZE, N_HEADS, HEAD_DIM), F32), pltpu.SemaphoreType.DMA((2,))]
    return out_spec, out_shape, scratch


def _prompt_ffn(x, n2, wu_c, cw_c, cb_c, wd_c, batch, seq, page_table_flat, cache_k, layer, first_page,
                pages_per_step):
    n = x.shape[0]
    tm = ROW_TILE
    tiles_per_seq = seq // tm
    n_steps = n // tm
    row_spec = pl.BlockSpec((tm, D_MODEL), lambda i, pt: (i, 0))
    st_shape = (2, N_FF_CHUNKS, CONV_W - 1, FF_CHUNK)
    bsum_spec, bsum_shape, page_scratch = _page_sum_specs(n_steps, pages_per_step, FFN_SUB_PAGES)
    return pl.pallas_call(
        functools.partial(_prompt_ffn_kernel, tiles_per_seq=tiles_per_seq, layer=layer,
                          first_page=first_page, pages_per_step=pages_per_step),
        grid_spec=pltpu.PrefetchScalarGridSpec(
            num_scalar_prefetch=1,
            grid=(n_steps,),
            in_specs=[row_spec, _const_spec((1, D_MODEL)),
                      _const_spec((2, N_FF_CHUNKS, D_MODEL, FF_CHUNK)),
                      _const_spec((2, N_FF_CHUNKS, CONV_W, FF_CHUNK)),
                      _const_spec((2, N_FF_CHUNKS, 1, FF_CHUNK)),
                      _const_spec((N_FF_CHUNKS, FF_CHUNK, D_MODEL)),
                      pl.BlockSpec(memory_space=pl.ANY)],
            out_specs=[row_spec,
                       pl.BlockSpec((1,) + st_shape, lambda i, pt: (i // tiles_per_seq, 0, 0, 0, 0)),
                       bsum_spec],
            scratch_shapes=[pltpu.VMEM((tm, D_MODEL), BF16), pltpu.VMEM(st_shape, F32)] + page_scratch,
        ),
        out_shape=[jax.ShapeDtypeStruct((n, D_MODEL), F32),
                   jax.ShapeDtypeStruct((batch,) + st_shape, F32),
                   bsum_shape],
        compiler_params=_params("arbitrary"),
        name="prompt_ffn",
    )(page_table_flat, x, n2, wu_c, cw_c, cb_c, wd_c, cache_k)


def _sample_in_kernel(x_ref, n1_ref, w_ref, vn_ref, qn_ref, kn_ref, wd0_ref, b0_ref,
                      ya_ref, q_ref, k_ref, vv_ref, gv_ref, sga_ref, sgb_ref):
    h = _rms(x_ref[...], n1_ref[...]).astype(BF16)

    def seg(s):
        return jnp.dot(h, w_ref[:, s * D_MODEL:(s + 1) * D_MODEL], preferred_element_type=F32)

    u = jax.nn.gelu(seg(0))
    v = _rms(jax.nn.gelu(seg(1)), vn_ref[...])
    gv_ref[...] = v
    ya_ref[...] = (u * (wd0_ref[...] * v + b0_ref[...])).astype(BF16)
    aq = seg(2)
    ak = seg(3)
    for hd in range(N_HEADS):
        cs = slice(hd * HEAD_DIM, (hd + 1) * HEAD_DIM)
        q_ref[:, cs] = _rms(aq[:, cs], qn_ref[...])
        k_ref[:, cs] = _rms(ak[:, cs], kn_ref[...])
    vv_ref[...] = seg(4)
    sga_ref[...] = jax.nn.sigmoid(seg(5))
    sgb_ref[...] = jax.nn.sigmoid(seg(6))


def _sample_in(x, n1, w_in, vn, qn, kn, wd0, b0):
    n = x.shape[0]
    f32_out = jax.ShapeDtypeStruct((n, D_MODEL), F32)
    return pl.pallas_call(
        _sample_in_kernel,
        out_shape=[jax.ShapeDtypeStruct((n, D_MODEL), BF16)] + [f32_out] * 6,
        compiler_params=pltpu.CompilerParams(vmem_limit_bytes=VMEM_LIMIT),
        name="sample_in",
    )(x, n1, w_in, vn, qn, kn, wd0, b0)


def _sample_gate_kernel(q_ref, k_ref, bsum_ref, idx_ref, *, n_past):
    q = q_ref[0]
    inv = 1.0 / MOBA_BLOCK
    past = (bsum_ref[0] * inv) * q
    own = (k_ref[0] * inv) * q
    prod = jnp.concatenate([past, own[None]], axis=0)
    ones = jnp.ones((HEAD_DIM, HEAD_DIM), F32)
    gate = jnp.dot(prod.reshape((n_past + 1) * N_HEADS, HEAD_DIM), ones,
                   precision=lax.Precision.HIGHEST, preferred_element_type=F32)
    gate = gate.reshape(n_past + 1, N_HEADS, HEAD_DIM)
    pos = lax.broadcasted_iota(jnp.int32, gate.shape, 0).astype(F32)
    lane = lax.broadcasted_iota(jnp.int32, (N_HEADS, HEAD_DIM), 1)
    out = jnp.zeros((N_HEADS, HEAD_DIM), jnp.int32)
    for r, first in enumerate(_top3_select(gate, pos, jnp.float32(n_past))):
        out = jnp.where(lane == r, first[0].astype(jnp.int32), out)
    idx_ref[0] = out


def _sample_gate(q, k, bsum):
    n = q.shape[0]
    n_past = bsum.shape[1]
    vec_spec = pl.BlockSpec((1, N_HEADS, HEAD_DIM), lambda b: (b, 0, 0))
    return pl.pallas_call(
        functools.partial(_sample_gate_kernel, n_past=n_past),
        grid=(n,),
        in_specs=[vec_spec, vec_spec,
                  pl.BlockSpec((1, n_past, N_HEADS, HEAD_DIM), lambda b: (b, 0, 0, 0))],
        out_specs=vec_spec,
        out_shape=jax.ShapeDtypeStruct((n, N_HEADS, HEAD_DIM), jnp.int32),
        compiler_params=_params("parallel"),
        name="sample_gate",
    )(q, k, bsum)


def _sample_attn_kernel(pt_ref, idx_ref, q_ref, kn_ref, vn_ref, ck_ref, cv_ref, o_ref,
                        kbuf, vbuf, sem, *, layer, n_pages):
    b = pl.program_id(0)
    n_seq = pl.num_programs(0)

    def page_copies(seq, slot):
        out = []
        for h in range(N_HEADS):
            for j in range(N_SEL_PAGES):
                r, p = divmod(j, PAGES_PER_BLOCK)
                blk = idx_ref[(seq * N_HEADS + h) * MOBA_TOPK + r]
                phys = pt_ref[seq * n_pages + blk * PAGES_PER_BLOCK + p]
                rows = pl.ds(j * PAGE_SIZE, PAGE_SIZE)
                out.append(pltpu.make_async_copy(ck_ref.at[layer, phys, :, h, :],
                                                 kbuf.at[slot, h, rows, :], sem.at[slot]))
                out.append(pltpu.make_async_copy(cv_ref.at[layer, phys, :, h, :],
                                                 vbuf.at[slot, h, rows, :], sem.at[slot]))
        return out

    @pl.when(b == 0)
    def _():
        for cp in page_copies(0, 0):
            cp.start()

    @pl.when(b + 1 < n_seq)
    def _():
        for cp in page_copies(b + 1, (b + 1) % 2):
            cp.start()

    slot = b % 2
    for cp in page_copies(b, slot):
        cp.wait()

    scale = HEAD_DIM ** -0.5
    for h in range(N_HEADS):
        q = q_ref[0, h:h + 1, :]
        s = jnp.sum(kbuf[slot, h] * q, axis=-1, keepdims=True) * scale
        s_own = jnp.sum(kn_ref[0, h:h + 1, :] * q, axis=-1, keepdims=True) * scale
        m = jnp.maximum(jnp.max(s, axis=0, keepdims=True), s_own)
        p = jnp.exp(s - m)
        p_own = jnp.exp(s_own - m)
        l = jnp.sum(p, axis=0, keepdims=True) + p_own
        acc = jnp.sum(p * vbuf[slot, h], axis=0, keepdims=True) + p_own * vn_ref[0, h:h + 1, :]
        o_ref[0, h:h + 1, :] = (acc / l).astype(o_ref.dtype)


def _sample_attn(q, k_new, v_new, cache_k, cache_v, page_table_flat, idx_flat, layer, n_pages):
    n = q.shape[0]
    vec_spec = pl.BlockSpec((1, N_HEADS, HEAD_DIM), lambda b, pt, ix: (b, 0, 0))
    any_spec = pl.BlockSpec(memory_space=pl.ANY)
    buf = pltpu.VMEM((2, N_HEADS, N_SEL_PAGES * PAGE_SIZE, HEAD_DIM), F32)
    return pl.pallas_call(
        functools.partial(_sample_attn_kernel, layer=layer, n_pages=n_pages),
        grid_spec=pltpu.PrefetchScalarGridSpec(
            num_scalar_prefetch=2,
            grid=(n,),
            in_specs=[vec_spec, vec_spec, vec_spec, any_spec, any_spec],
            out_specs=vec_spec,
            scratch_shapes=[buf, buf, pltpu.SemaphoreType.DMA((2,))],
        ),
        out_shape=jax.ShapeDtypeStruct((n, N_HEADS, HEAD_DIM), BF16),
        compiler_params=_params("arbitrary"),
        name="sample_attn",
    )(page_table_flat, idx_flat, q, k_new, v_new, cache_k, cache_v)


def _sample_out_kernel(x_ref, ya_ref, yb_ref, sga_ref, sgb_ref, wba_ref, wbb_ref, wout_ref,
                       n2_ref, wu_ref, cw_ref, cb_ref, wd_ref, h0_ref, h1_ref, o_ref, up_ref):
    a = jnp.dot(ya_ref[...], wba_ref[...], preferred_element_type=F32)
    b = jnp.dot(yb_ref[...], wbb_ref[...], preferred_element_type=F32)
    mix = (sga_ref[...] * a + sgb_ref[...] * b).astype(BF16)
    x = x_ref[...] + jnp.dot(mix, wout_ref[...], preferred_element_type=F32)
    h = _rms(x, n2_ref[...]).astype(BF16)
    up = jnp.dot(h, wu_ref[...], preferred_element_type=F32)
    up_ref[...] = up
    c = cb_ref[...] + cw_ref[0:1, :] * h0_ref[...] + cw_ref[1:2, :] * h1_ref[...] + cw_ref[2:3, :] * up
    act = (jax.nn.gelu(c[:, :D_FF]) * c[:, D_FF:]).astype(BF16)
    o_ref[...] = x + jnp.dot(act, wd_ref[...], preferred_element_type=F32)


def _sample_out(x, ya, yb, sga, sgb, wba, wbb, wout, n2, wu, cw, cb, wd, h0, h1):
    n = x.shape[0]
    return pl.pallas_call(
        _sample_out_kernel,
        out_shape=[jax.ShapeDtypeStruct((n, D_MODEL), F32), jax.ShapeDtypeStruct((n, 2 * D_FF), F32)],
        compiler_params=pltpu.CompilerParams(vmem_limit_bytes=VMEM_LIMIT),
        name="sample_out",
    )(x, ya, yb, sga, sgb, wba, wbb, wout, n2, wu, cw, cb, wd, h0, h1)


def _chunk_cols(a):
    lead = a.shape[:-1]
    a = a.reshape(lead + (2, N_FF_CHUNKS, FF_CHUNK))
    return jnp.moveaxis(a, (-3, -2), (0, 1))


def kernel(x_prompt, x_sample, cache_k, cache_v, state_conv, page_table, norm1_w, w_in, gmlp_vnorm_w, gmlp_ws, gmlp_bs, q_norm_w, k_norm_w, w_branch_a, w_branch_b, w_out, norm2_w, w_up, conv_w, conv_b, w_down):
    batch, seq, _ = x_prompt.shape
    n_seq, dec_seq, _ = x_sample.shape
    depth = w_in.shape[0]
    n_pages = page_table.shape[1]
    assert dec_seq == 1 and seq % ROW_TILE == 0 and ROW_TILE % MOBA_BLOCK == 0
    assert n_pages % PAGES_PER_BLOCK == 0
    n_tiles = batch * seq // ROW_TILE
    assert (n_seq * n_pages) % (n_tiles * IN_PAGE_SHARE) == 0
    in_pages = n_seq * n_pages // (n_tiles * IN_PAGE_SHARE)
    ffn_pages = n_seq * n_pages // n_tiles - in_pages
    ffn_first = n_tiles * in_pages

    pt_flat = page_table.reshape(-1)
    heads = lambda a: a.reshape(n_seq, N_HEADS, HEAD_DIM)

    xp = x_prompt.reshape(batch * seq, D_MODEL)
    xs = x_sample.reshape(n_seq, D_MODEL)
    row = lambda a: a.reshape(1, -1)
    k_p, v_p, k_s, v_s, gv_s, cv_p, cv_s = [], [], [], [], [], [], []
    for l in range(depth):
        w_in_b = w_in[l].astype(BF16)
        wba, wbb, wout = w_branch_a[l].astype(BF16), w_branch_b[l].astype(BF16), w_out[l].astype(BF16)
        wu_b, wd_b = w_up[l].astype(BF16), w_down[l].astype(BF16)
        n1, n2, vn, qn, kn = row(norm1_w[l]), row(norm2_w[l]), row(gmlp_vnorm_w[l]), row(q_norm_w[l]), row(k_norm_w[l])

        ya, q, k, vv, kb, vt, ksum, sga, sgb, bsum_a = _prompt_in(
            xp, n1, w_in_b, vn, qn, kn, gmlp_ws[l], gmlp_bs[l].T, pt_flat, cache_k, l, 0, in_pages)
        yb = _prompt_attn(q, kb, vt, ksum, batch, seq)
        xp = _merge(xp, ya, yb, sga, sgb, wba, wbb, wout, ROW_TILE)
        xp, st, bsum_b = _prompt_ffn(xp, n2, _chunk_cols(wu_b), _chunk_cols(conv_w[l]),
                                     _chunk_cols(conv_b[l].reshape(1, -1)),
                                     wd_b.reshape(N_FF_CHUNKS, FF_CHUNK, D_MODEL), batch, seq,
                                     pt_flat, cache_k, l, ffn_first, ffn_pages)
        bsum = jnp.concatenate([bsum_a, bsum_b], axis=0).reshape(
            n_seq, n_pages // PAGES_PER_BLOCK, N_HEADS, HEAD_DIM)
        k_p.append(k.reshape(batch, seq, N_HEADS, HEAD_DIM))
        v_p.append(vv.reshape(batch, seq, N_HEADS, HEAD_DIM))
        cv_p.append(jnp.transpose(st, (0, 3, 1, 2, 4)).reshape(batch, CONV_W - 1, 2 * D_FF))

        wd0 = jnp.repeat(gmlp_ws[l][:, 0, 0], GROUP_DIM).reshape(1, -1)
        b0 = jnp.repeat(gmlp_bs[l][:, 0], GROUP_DIM).reshape(1, -1)
        ya, q, k, vv, gv, sga, sgb = _sample_in(xs, n1, w_in_b, vn, qn, kn, wd0, b0)
        idx = _sample_gate(heads(q), heads(k), bsum)
        idx_flat = idx[:, :, :MOBA_TOPK].reshape(-1)
        yb = _sample_attn(heads(q), heads(k), heads(vv), cache_k, cache_v, pt_flat, idx_flat, l, n_pages)
        yb = yb.reshape(n_seq, D_MODEL)
        hist = state_conv[l]
        xs, up = _sample_out(xs, ya, yb, sga, sgb, wba, wbb, wout, n2, wu_b, conv_w[l],
                             conv_b[l].reshape(1, -1), wd_b, hist[:, 0, :], hist[:, 1, :])
        k_s.append(k.reshape(n_seq, 1, N_HEADS, HEAD_DIM))
        v_s.append(vv.reshape(n_seq, 1, N_HEADS, HEAD_DIM))
        gv_s.append(gv.reshape(n_seq, 1, D_MODEL))
        cv_s.append(jnp.stack([hist[:, 1, :], up], axis=1))

    return (xp.reshape(batch, seq, D_MODEL), xs.reshape(n_seq, 1, D_MODEL),
            jnp.stack(k_p), jnp.stack(v_p), jnp.stack(k_s), jnp.stack(v_s),
            jnp.stack(gv_s), jnp.stack(cv_p), jnp.stack(cv_s))
```

```python
import functools

import jax
import jax.numpy as jnp
from jax import lax
from jax.experimental import pallas as pl
from jax.experimental.pallas import tpu as pltpu

F32 = jnp.float32
BF16 = jnp.bfloat16

D_MODEL = 1024
N_HEADS = 8
HEAD_DIM = 128
GMLP_GROUPS = 8
GROUP_DIM = 128
CHUNK = 128
MOBA_BLOCK = 256
MOBA_TOPK = 3
D_FF = 2816
CONV_W = 3
EPS = 1e-6
PAGE_SIZE = 128
N_SEG = 7
FF_CHUNK = 256
N_FF_CHUNKS = D_FF // FF_CHUNK
PAGE_SUB_BATCH = 8
PAGE_RING_SLOTS = 4
SUM_CHAINS = 8
VT_ROWS = HEAD_DIM + 16
PAGES_PER_BLOCK = MOBA_BLOCK // PAGE_SIZE
N_SEL_PAGES = MOBA_TOPK * PAGES_PER_BLOCK
ROW_TILE = 512
ATTN_HEADS = 4
LOG2_E = 1.4426950408889634
VMEM_LIMIT = 60 * 1024 * 1024


def _rms(x, w):
    ms = jnp.mean(x * x, axis=-1, keepdims=True)
    return x * lax.rsqrt(ms + EPS) * w


def _const_spec(shape):
    nd = len(shape)
    return pl.BlockSpec(shape, lambda *_: (0,) * nd, pipeline_mode=pl.Buffered(1))


def _params(*sem):
    return pltpu.CompilerParams(dimension_semantics=sem, vmem_limit_bytes=VMEM_LIMIT)


def _top3_select(gate, pos, n_valid):
    neg = jnp.float32(-jnp.inf)
    gate = jnp.where(pos < n_valid, gate, neg)
    picks = []
    for _ in range(MOBA_TOPK):
        m = jnp.max(gate, axis=0, keepdims=True)
        first = jnp.min(jnp.where(gate == m, pos, jnp.float32(2 ** 30)), axis=0, keepdims=True)
        picks.append(first)
        gate = jnp.where(pos == first, neg, gate)
    return picks


class _PageSums:
    def __init__(self, pt_ref, cache_ref, out_ref, buf, sem, *, layer, first_page, pages_per_step):
        self.pt_ref, self.cache_ref, self.out_ref, self.buf, self.sem = pt_ref, cache_ref, out_ref, buf, sem
        self.layer, self.first_page = layer, first_page
        self.slots, self.sub = buf.shape[0], buf.shape[1]
        self.n_sub = pages_per_step // self.sub
        assert pages_per_step % self.sub == 0 and self.n_sub % self.slots == 0
        assert self.sub % PAGES_PER_BLOCK == 0
        self.step = pl.program_id(0)
        self.n_steps = pl.num_programs(0)

    def _copies(self, step, s):
        slot = s % self.slots
        base = self.first_page + (step * self.n_sub + s) * self.sub
        return [pltpu.make_async_copy(self.cache_ref.at[self.layer, self.pt_ref[base + p]],
                                      self.buf.at[slot, p], self.sem.at[slot]) for p in range(self.sub)]

    def _start(self, step, s):
        for cp in self._copies(step, s):
            cp.start()

    def prime(self):
        @pl.when(self.step == 0)
        def _():
            for s in range(self.slots):
                self._start(0, s)

    def _wait(self, s):
        for cp in self._copies(self.step, s):
            cp.wait()

    def _reduce(self, s):
        slot = s % self.slots
        per_sub = self.sub // PAGES_PER_BLOCK
        lanes_of = (PAGE_SIZE // SUM_CHAINS, SUM_CHAINS, N_HEADS, HEAD_DIM)
        for r in range(per_sub):
            part = jnp.sum(self.buf[slot, PAGES_PER_BLOCK * r].reshape(lanes_of), axis=0)
            for p in range(1, PAGES_PER_BLOCK):
                part = part + jnp.sum(self.buf[slot, PAGES_PER_BLOCK * r + p].reshape(lanes_of), axis=0)
            self.out_ref[s * per_sub + r] = jnp.sum(part, axis=0)

    def _refill(self, s):
        if s + self.slots < self.n_sub:
            self._start(self.step, s + self.slots)
        else:
            @pl.when(self.step + 1 < self.n_steps)
            def _():
                self._start(self.step + 1, s + self.slots - self.n_sub)

    def _at(self, k, n_points):
        return [s for s in range(self.n_sub) if s * n_points // self.n_sub == k]

    def wait_point(self, k, n_points):
        for s in self._at(k, n_points):
            self._wait(s)

    def reduce_point(self, k, n_points):
        for s in self._at(k, n_points):
            self._reduce(s)

    def after_point(self, k, n_points):
        for s in self._at(k, n_points):
            self._refill(s)


def _prompt_in_kernel(x_ref, n1_ref, w_ref, vn_ref, qn_ref, kn_ref, ws_ref, bst_ref,
                      ya_ref, q_ref, k_ref, vv_ref, kb_ref, vt_ref, ksum_ref, sga_ref, sgb_ref):
    tm = x_ref.shape[0]
    h = _rms(x_ref[...], n1_ref[...]).astype(BF16)

    def seg(s):
        return jnp.dot(h, w_ref[:, s * D_MODEL:(s + 1) * D_MODEL], preferred_element_type=F32)

    u = jax.nn.gelu(seg(0))
    v = _rms(jax.nn.gelu(seg(1)), vn_ref[...]).astype(BF16)
    row = lax.broadcasted_iota(jnp.int32, (CHUNK, CHUNK), 0)
    col = lax.broadcasted_iota(jnp.int32, (CHUNK, CHUNK), 1)
    for g in range(GMLP_GROUPS):
        wg = jnp.where(row >= col, ws_ref[g], 0.0).astype(BF16)
        bg = bst_ref[:, g:g + 1]
        cs = slice(g * GROUP_DIM, (g + 1) * GROUP_DIM)
        for c in range(tm // CHUNK):
            rs = slice(c * CHUNK, (c + 1) * CHUNK)
            mixed = jnp.dot(wg, v[rs, cs], preferred_element_type=F32) + bg
            ya_ref[rs, cs] = (u[rs, cs] * mixed).astype(BF16)

    aq = seg(2)
    for hd in range(N_HEADS):
        cs = slice(hd * HEAD_DIM, (hd + 1) * HEAD_DIM)
        q_ref[:, cs] = _rms(aq[:, cs], qn_ref[...])
    ak = seg(3)
    for hd in range(N_HEADS):
        cs = slice(hd * HEAD_DIM, (hd + 1) * HEAD_DIM)
        kh = _rms(ak[:, cs], kn_ref[...])
        k_ref[:, cs] = kh
        kb_ref[:, cs] = kh.astype(BF16)
    for blk in range(tm // MOBA_BLOCK):
        rs = slice(blk * MOBA_BLOCK, (blk + 1) * MOBA_BLOCK)
        ksum_ref[0, blk:blk + 1, :] = jnp.sum(k_ref[rs, :], axis=0, keepdims=True)
    av = seg(4)
    vv_ref[...] = av
    ones = jnp.ones((VT_ROWS - HEAD_DIM, MOBA_BLOCK), BF16)
    for blk in range(tm // MOBA_BLOCK):
        rs = slice(blk * MOBA_BLOCK, (blk + 1) * MOBA_BLOCK)
        for hd in range(N_HEADS):
            vt_ref[blk, hd, 0:HEAD_DIM, :] = av[rs, hd * HEAD_DIM:(hd + 1) * HEAD_DIM].T.astype(BF16)
            vt_ref[blk, hd, HEAD_DIM:VT_ROWS, :] = ones
    sga_ref[...] = jax.nn.sigmoid(seg(5))
    sgb_ref[...] = jax.nn.sigmoid(seg(6))


def _prompt_in(x, n1, w_in, vn, qn, kn, ws, bst):
    n = x.shape[0]
    tm = ROW_TILE
    nt = n // tm
    row_spec = pl.BlockSpec((tm, D_MODEL), lambda i: (i, 0))
    f32_out = jax.ShapeDtypeStruct((n, D_MODEL), F32)
    bf_out = jax.ShapeDtypeStruct((n, D_MODEL), BF16)
    return pl.pallas_call(
        _prompt_in_kernel,
        grid=(nt,),
        in_specs=[row_spec, _const_spec((1, D_MODEL)), _const_spec((D_MODEL, N_SEG * D_MODEL)),
                  _const_spec((1, D_MODEL)), _const_spec((1, HEAD_DIM)), _const_spec((1, HEAD_DIM)),
                  _const_spec((GMLP_GROUPS, CHUNK, CHUNK)), _const_spec((CHUNK, GMLP_GROUPS))],
        out_specs=[row_spec, row_spec, row_spec, row_spec, row_spec,
                   pl.BlockSpec((tm // MOBA_BLOCK, N_HEADS, VT_ROWS, MOBA_BLOCK), lambda i: (i, 0, 0, 0)),
                   pl.BlockSpec((1, tm // MOBA_BLOCK, D_MODEL), lambda i: (i, 0, 0)),
                   row_spec, row_spec],
        out_shape=[bf_out, f32_out, f32_out, f32_out, bf_out,
                   jax.ShapeDtypeStruct((n // MOBA_BLOCK, N_HEADS, VT_ROWS, MOBA_BLOCK), BF16),
                   jax.ShapeDtypeStruct((nt, tm // MOBA_BLOCK, D_MODEL), F32),
                   f32_out, f32_out],
        compiler_params=_params("parallel"),
        name="prompt_in",
    )(x, n1, w_in, vn, qn, kn, ws, bst)


def _prompt_attn_kernel(q_ref, kb_ref, vt_ref, ksum_ref, o_ref,
                        qt_scr, sel_scr, m_scr, acc_scr, p_scr, s_even, s_odd):
    i = pl.program_id(2)
    nb = ksum_ref.shape[1]
    blk = MOBA_BLOCK
    c = HEAD_DIM ** -0.5 * LOG2_E
    neg = jnp.float32(-jnp.inf)
    heads = range(ATTN_HEADS)
    cols = [slice(g * HEAD_DIM, (g + 1) * HEAD_DIM) for g in heads]
    pos = lax.broadcasted_iota(jnp.int32, (nb, blk), 0).astype(F32)
    n_past = i.astype(F32)

    def scores(j, dst):
        rows = pl.ds(pl.multiple_of(j * blk, blk), blk)
        for g in heads:
            dst[g] = jnp.dot(kb_ref[rows, cols[g]], qt_scr[g], preferred_element_type=F32)

    def pv(j, alphas):
        for g in heads:
            upd = jnp.dot(vt_ref[j, g], p_scr[g], preferred_element_type=F32)
            acc_scr[g] = upd if alphas is None else alphas[g] * acc_scr[g] + upd

    for g in heads:
        qt = q_ref[:, cols[g]].T
        qt_scr[g] = (qt * c).astype(BF16)
        kbar = ksum_ref[0, :, cols[g]] * (1.0 / blk)
        gate = jnp.dot(kbar, qt, precision=lax.Precision.HIGHEST, preferred_element_type=F32)
        sel = jnp.zeros((nb, blk), F32)
        for first in _top3_select(gate, pos, n_past):
            sel = jnp.where((pos == first) & (pos < n_past), 1.0, sel)
        sel_scr[g] = sel

    scores(i, s_odd)
    scores(0, s_even)
    key = lax.broadcasted_iota(jnp.int32, (blk, blk), 0)
    qry = lax.broadcasted_iota(jnp.int32, (blk, blk), 1)
    for g in heads:
        s = jnp.where(key <= qry, s_odd[g], neg)
        m = jnp.max(s, axis=0, keepdims=True)
        p = jnp.exp2(s - m)
        m_scr[g] = m
        p_scr[g] = p.astype(BF16)
    pv(i, None)

    def past_block(j, src, dst):
        scores(jnp.minimum(j + 1, nb - 1), dst)
        alphas = []
        for g in heads:
            s = src[g]
            chosen = sel_scr[g, pl.ds(j, 1), :] > 0.5
            m_old = m_scr[g]
            m_new = jnp.where(chosen, jnp.maximum(m_old, jnp.max(s, axis=0, keepdims=True)), m_old)
            p = jnp.exp2(s - jnp.where(chosen, m_new, jnp.float32(jnp.inf)))
            alpha = jnp.exp2(m_old - m_new)
            m_scr[g] = m_new
            p_scr[g] = p.astype(BF16)
            alphas.append(alpha)
        pv(j, alphas)

    def two_blocks(t, _):
        past_block(2 * t, s_even, s_odd)
        past_block(2 * t + 1, s_odd, s_even)
        return 0

    lax.fori_loop(0, (i + 1) // 2, two_blocks, 0)
    for g in heads:
        out = acc_scr[g, 0:HEAD_DIM, :] / acc_scr[g, HEAD_DIM:HEAD_DIM + 1, :]
        o_ref[:, cols[g]] = out.T.astype(o_ref.dtype)


def _prompt_attn(q, kb, vt, ksum, batch, seq):
    n = q.shape[0]
    nb = seq // MOBA_BLOCK
    gw = ATTN_HEADS * HEAD_DIM
    ksum = ksum.reshape(batch, nb, D_MODEL)
    return pl.pallas_call(
        _prompt_attn_kernel,
        grid=(batch, N_HEADS // ATTN_HEADS, nb),
        in_specs=[pl.BlockSpec((MOBA_BLOCK, gw), lambda b, h, i: (b * nb + i, h)),
                  pl.BlockSpec((seq, gw), lambda b, h, i: (b, h)),
                  pl.BlockSpec((nb, ATTN_HEADS, VT_ROWS, MOBA_BLOCK), lambda b, h, i: (b, h, 0, 0)),
                  pl.BlockSpec((1, nb, gw), lambda b, h, i: (b, 0, h))],
        out_specs=pl.BlockSpec((MOBA_BLOCK, gw), lambda b, h, i: (b * nb + i, h)),
        out_shape=jax.ShapeDtypeStruct((n, D_MODEL), BF16),
        scratch_shapes=[pltpu.VMEM((ATTN_HEADS, HEAD_DIM, MOBA_BLOCK), BF16),
                        pltpu.VMEM((ATTN_HEADS, nb, MOBA_BLOCK), F32),
                        pltpu.VMEM((ATTN_HEADS, 1, MOBA_BLOCK), F32),
                        pltpu.VMEM((ATTN_HEADS, VT_ROWS, MOBA_BLOCK), F32),
                        pltpu.VMEM((ATTN_HEADS, MOBA_BLOCK, MOBA_BLOCK), BF16),
                        pltpu.VMEM((ATTN_HEADS, MOBA_BLOCK, MOBA_BLOCK), F32),
                        pltpu.VMEM((ATTN_HEADS, MOBA_BLOCK, MOBA_BLOCK), F32)],
        compiler_params=_params("parallel", "parallel", "arbitrary"),
        name="prompt_attn",
    )(q, kb, vt, ksum)


def _merge_kernel(x_ref, ya_ref, yb_ref, sga_ref, sgb_ref, wba_ref, wbb_ref, wout_ref, o_ref):
    a = jnp.dot(ya_ref[...], wba_ref[...], preferred_element_type=F32)
    b = jnp.dot(yb_ref[...], wbb_ref[...], preferred_element_type=F32)
    mix = (sga_ref[...] * a + sgb_ref[...] * b).astype(BF16)
    o_ref[...] = x_ref[...] + jnp.dot(mix, wout_ref[...], preferred_element_type=F32)


def _merge(x, ya, yb, sga, sgb, wba, wbb, wout, tm):
    n = x.shape[0]
    row_spec = pl.BlockSpec((tm, D_MODEL), lambda i: (i, 0))
    w_spec = _const_spec((D_MODEL, D_MODEL))
    return pl.pallas_call(
        _merge_kernel,
        grid=(n // tm,),
        in_specs=[row_spec] * 5 + [w_spec] * 3,
        out_specs=row_spec,
        out_shape=jax.ShapeDtypeStruct((n, D_MODEL), F32),
        compiler_params=_params("parallel"),
        name="merge",
    )(x, ya, yb, sga, sgb, wba, wbb, wout)


def _prompt_ffn_kernel(pt_ref, x_ref, n2_ref, wu_ref, cw_ref, cb_ref, wd_ref, cache_ref,
                       o_ref, st_ref, bsum_ref, h_scr, carry_scr, page_buf, page_sem,
                       *, tiles_per_seq, layer, first_page, pages_per_step):
    tm = x_ref.shape[0]
    pages = _PageSums(pt_ref, cache_ref, bsum_ref, page_buf, page_sem, layer=layer,
                      first_page=first_page, pages_per_step=pages_per_step)
    pages.prime()

    @pl.when(pl.program_id(0) % tiles_per_seq == 0)
    def _():
        carry_scr[...] = jnp.zeros_like(carry_scr)

    x = x_ref[...]
    h_scr[...] = _rms(x, n2_ref[...]).astype(BF16)
    row = lax.broadcasted_iota(jnp.int32, (8, FF_CHUNK), 0)

    def up_proj(part, c):
        if c >= N_FF_CHUNKS:
            return None
        return jnp.dot(h_scr[...], wu_ref[part, c], preferred_element_type=F32)

    def conv(up, part, c):
        prev = carry_scr[part, c]
        r1, r2 = pltpu.roll(up, 1, 0), pltpu.roll(up, 2, 0)
        top1 = jnp.where(row == 0, prev[1:2, :], r1[0:8, :])
        top2 = jnp.where(row == 0, prev[0:1, :], jnp.where(row == 1, prev[1:2, :], r2[0:8, :]))
        u1 = jnp.concatenate([top1, r1[8:, :]], axis=0)
        u2 = jnp.concatenate([top2, r2[8:, :]], axis=0)
        carry_scr[part, c] = up[tm - 2:tm, :]
        cw = cw_ref[part, c]
        return cb_ref[part, c] + cw[0:1, :] * u2 + cw[1:2, :] * u1 + cw[2:3, :] * up

    acc = x
    up_g, up_z = up_proj(0, 0), up_proj(1, 0)
    for c in range(N_FF_CHUNKS):
        pages.wait_point(c, N_FF_CHUNKS)
        next_g = up_proj(0, c + 1)
        gate = jax.nn.gelu(conv(up_g, 0, c))
        next_z = up_proj(1, c + 1)
        act = (gate * conv(up_z, 1, c)).astype(BF16)
        acc = acc + jnp.dot(act, wd_ref[c], preferred_element_type=F32)
        pages.reduce_point(c, N_FF_CHUNKS)
        pages.after_point(c, N_FF_CHUNKS)
        up_g, up_z = next_g, next_z
    o_ref[...] = acc
    st_ref[0] = carry_scr[...]


def _page_sum_specs(n_steps, pages_per_step):
    blocks = pages_per_step // PAGES_PER_BLOCK
    out_spec = pl.BlockSpec((blocks, N_HEADS, HEAD_DIM), lambda i, pt: (i, 0, 0))
    out_shape = jax.ShapeDtypeStruct((n_steps * blocks, N_HEADS, HEAD_DIM), F32)
    scratch = [pltpu.VMEM((PAGE_RING_SLOTS, PAGE_SUB_BATCH, PAGE_SIZE, N_HEADS, HEAD_DIM), F32),
               pltpu.SemaphoreType.DMA((PAGE_RING_SLOTS,))]
    return out_spec, out_shape, scratch


def _prompt_ffn(x, n2, wu_c, cw_c, cb_c, wd_c, batch, seq, page_table_flat, cache_k, layer, first_page,
                pages_per_step):
    n = x.shape[0]
    tm = ROW_TILE
    tiles_per_seq = seq // tm
    n_steps = n // tm
    row_spec = pl.BlockSpec((tm, D_MODEL), lambda i, pt: (i, 0))
    st_shape = (2, N_FF_CHUNKS, CONV_W - 1, FF_CHUNK)
    bsum_spec, bsum_shape, page_scratch = _page_sum_specs(n_steps, pages_per_step)
    return pl.pallas_call(
        functools.partial(_prompt_ffn_kernel, tiles_per_seq=tiles_per_seq, layer=layer,
                          first_page=first_page, pages_per_step=pages_per_step),
        grid_spec=pltpu.PrefetchScalarGridSpec(
            num_scalar_prefetch=1,
            grid=(n_steps,),
            in_specs=[row_spec, _const_spec((1, D_MODEL)),
                      _const_spec((2, N_FF_CHUNKS, D_MODEL, FF_CHUNK)),
                      _const_spec((2, N_FF_CHUNKS, CONV_W, FF_CHUNK)),
                      _const_spec((2, N_FF_CHUNKS, 1, FF_CHUNK)),
                      _const_spec((N_FF_CHUNKS, FF_CHUNK, D_MODEL)),
                      pl.BlockSpec(memory_space=pl.ANY)],
            out_specs=[row_spec,
                       pl.BlockSpec((1,) + st_shape, lambda i, pt: (i // tiles_per_seq, 0, 0, 0, 0)),
                       bsum_spec],
            scratch_shapes=[pltpu.VMEM((tm, D_MODEL), BF16), pltpu.VMEM(st_shape, F32)] + page_scratch,
        ),
        out_shape=[jax.ShapeDtypeStruct((n, D_MODEL), F32),
                   jax.ShapeDtypeStruct((batch,) + st_shape, F32),
                   bsum_shape],
        compiler_params=_params("arbitrary"),
        name="prompt_ffn",
    )(page_table_flat, x, n2, wu_c, cw_c, cb_c, wd_c, cache_k)


def _sample_in_kernel(x_ref, n1_ref, w_ref, vn_ref, qn_ref, kn_ref, wd0_ref, b0_ref,
                      ya_ref, q_ref, k_ref, vv_ref, gv_ref, sga_ref, sgb_ref):
    h = _rms(x_ref[...], n1_ref[...]).astype(BF16)

    def seg(s):
        return jnp.dot(h, w_ref[:, s * D_MODEL:(s + 1) * D_MODEL], preferred_element_type=F32)

    u = jax.nn.gelu(seg(0))
    v = _rms(jax.nn.gelu(seg(1)), vn_ref[...])
    gv_ref[...] = v
    ya_ref[...] = (u * (wd0_ref[...] * v + b0_ref[...])).astype(BF16)
    aq = seg(2)
    ak = seg(3)
    for hd in range(N_HEADS):
        cs = slice(hd * HEAD_DIM, (hd + 1) * HEAD_DIM)
        q_ref[:, cs] = _rms(aq[:, cs], qn_ref[...])
        k_ref[:, cs] = _rms(ak[:, cs], kn_ref[...])
    vv_ref[...] = seg(4)
    sga_ref[...] = jax.nn.sigmoid(seg(5))
    sgb_ref[...] = jax.nn.sigmoid(seg(6))


def _sample_in(x, n1, w_in, vn, qn, kn, wd0, b0):
    n = x.shape[0]
    f32_out = jax.ShapeDtypeStruct((n, D_MODEL), F32)
    return pl.pallas_call(
        _sample_in_kernel,
        out_shape=[jax.ShapeDtypeStruct((n, D_MODEL), BF16)] + [f32_out] * 6,
        compiler_params=pltpu.CompilerParams(vmem_limit_bytes=VMEM_LIMIT),
        name="sample_in",
    )(x, n1, w_in, vn, qn, kn, wd0, b0)


def _sample_gate_kernel(q_ref, k_ref, bsum_ref, idx_ref, *, n_past):
    q = q_ref[0]
    inv = 1.0 / MOBA_BLOCK
    past = (bsum_ref[0] * inv) * q
    own = (k_ref[0] * inv) * q
    prod = jnp.concatenate([past, own[None]], axis=0)
    ones = jnp.ones((HEAD_DIM, HEAD_DIM), F32)
    gate = jnp.dot(prod.reshape((n_past + 1) * N_HEADS, HEAD_DIM), ones,
                   precision=lax.Precision.HIGHEST, preferred_element_type=F32)
    gate = gate.reshape(n_past + 1, N_HEADS, HEAD_DIM)
    pos = lax.broadcasted_iota(jnp.int32, gate.shape, 0).astype(F32)
    lane = lax.broadcasted_iota(jnp.int32, (N_HEADS, HEAD_DIM), 1)
    out = jnp.zeros((N_HEADS, HEAD_DIM), jnp.int32)
    for r, first in enumerate(_top3_select(gate, pos, jnp.float32(n_past))):
        out = jnp.where(lane == r, first[0].astype(jnp.int32), out)
    idx_ref[0] = out


def _sample_gate(q, k, bsum):
    n = q.shape[0]
    n_past = bsum.shape[1]
    vec_spec = pl.BlockSpec((1, N_HEADS, HEAD_DIM), lambda b: (b, 0, 0))
    return pl.pallas_call(
        functools.partial(_sample_gate_kernel, n_past=n_past),
        grid=(n,),
        in_specs=[vec_spec, vec_spec,
                  pl.BlockSpec((1, n_past, N_HEADS, HEAD_DIM), lambda b: (b, 0, 0, 0))],
        out_specs=vec_spec,
        out_shape=jax.ShapeDtypeStruct((n, N_HEADS, HEAD_DIM), jnp.int32),
        compiler_params=_params("parallel"),
        name="sample_gate",
    )(q, k, bsum)


def _sample_attn_kernel(pt_ref, idx_ref, q_ref, kn_ref, vn_ref, ck_ref, cv_ref, o_ref,
                        kbuf, vbuf, sem, *, layer, n_pages):
    b = pl.program_id(0)
    n_seq = pl.num_programs(0)

    def page_copies(seq, slot):
        out = []
        for h in range(N_HEADS):
            for j in range(N_SEL_PAGES):
                r, p = divmod(j, PAGES_PER_BLOCK)
                blk = idx_ref[(seq * N_HEADS + h) * MOBA_TOPK + r]
                phys = pt_ref[seq * n_pages + blk * PAGES_PER_BLOCK + p]
                rows = pl.ds(j * PAGE_SIZE, PAGE_SIZE)
                out.append(pltpu.make_async_copy(ck_ref.at[layer, phys, :, h, :],
                                                 kbuf.at[slot, h, rows, :], sem.at[slot]))
                out.append(pltpu.make_async_copy(cv_ref.at[layer, phys, :, h, :],
                                                 vbuf.at[slot, h, rows, :], sem.at[slot]))
        return out

    @pl.when(b == 0)
    def _():
        for cp in page_copies(0, 0):
            cp.start()

    @pl.when(b + 1 < n_seq)
    def _():
        for cp in page_copies(b + 1, (b + 1) % 2):
            cp.start()

    slot = b % 2
    for cp in page_copies(b, slot):
        cp.wait()

    scale = HEAD_DIM ** -0.5
    for h in range(N_HEADS):
        q = q_ref[0, h:h + 1, :]
        s = jnp.sum(kbuf[slot, h] * q, axis=-1, keepdims=True) * scale
        s_own = jnp.sum(kn_ref[0, h:h + 1, :] * q, axis=-1, keepdims=True) * scale
        m = jnp.maximum(jnp.max(s, axis=0, keepdims=True), s_own)
        p = jnp.exp(s - m)
        p_own = jnp.exp(s_own - m)
        l = jnp.sum(p, axis=0, keepdims=True) + p_own
        acc = jnp.sum(p * vbuf[slot, h], axis=0, keepdims=True) + p_own * vn_ref[0, h:h + 1, :]
        o_ref[0, h:h + 1, :] = (acc / l).astype(o_ref.dtype)


def _sample_attn(q, k_new, v_new, cache_k, cache_v, page_table_flat, idx_flat, layer, n_pages):
    n = q.shape[0]
    vec_spec = pl.BlockSpec((1, N_HEADS, HEAD_DIM), lambda b, pt, ix: (b, 0, 0))
    any_spec = pl.BlockSpec(memory_space=pl.ANY)
    buf = pltpu.VMEM((2, N_HEADS, N_SEL_PAGES * PAGE_SIZE, HEAD_DIM), F32)
    return pl.pallas_call(
        functools.partial(_sample_attn_kernel, layer=layer, n_pages=n_pages),
        grid_spec=pltpu.PrefetchScalarGridSpec(
            num_scalar_prefetch=2,
            grid=(n,),
            in_specs=[vec_spec, vec_spec, vec_spec, any_spec, any_spec],
            out_specs=vec_spec,
            scratch_shapes=[buf, buf, pltpu.SemaphoreType.DMA((2,))],
        ),
        out_shape=jax.ShapeDtypeStruct((n, N_HEADS, HEAD_DIM), BF16),
        compiler_params=_params("arbitrary"),
        name="sample_attn",
    )(page_table_flat, idx_flat, q, k_new, v_new, cache_k, cache_v)


def _sample_out_kernel(x_ref, ya_ref, yb_ref, sga_ref, sgb_ref, wba_ref, wbb_ref, wout_ref,
                       n2_ref, wu_ref, cw_ref, cb_ref, wd_ref, h0_ref, h1_ref, o_ref, up_ref):
    a = jnp.dot(ya_ref[...], wba_ref[...], preferred_element_type=F32)
    b = jnp.dot(yb_ref[...], wbb_ref[...], preferred_element_type=F32)
    mix = (sga_ref[...] * a + sgb_ref[...] * b).astype(BF16)
    x = x_ref[...] + jnp.dot(mix, wout_ref[...], preferred_element_type=F32)
    h = _rms(x, n2_ref[...]).astype(BF16)
    up = jnp.dot(h, wu_ref[...], preferred_element_type=F32)
    up_ref[...] = up
    c = cb_ref[...] + cw_ref[0:1, :] * h0_ref[...] + cw_ref[1:2, :] * h1_ref[...] + cw_ref[2:3, :] * up
    act = (jax.nn.gelu(c[:, :D_FF]) * c[:, D_FF:]).astype(BF16)
    o_ref[...] = x + jnp.dot(act, wd_ref[...], preferred_element_type=F32)


def _sample_out(x, ya, yb, sga, sgb, wba, wbb, wout, n2, wu, cw, cb, wd, h0, h1):
    n = x.shape[0]
    return pl.pallas_call(
        _sample_out_kernel,
        out_shape=[jax.ShapeDtypeStruct((n, D_MODEL), F32), jax.ShapeDtypeStruct((n, 2 * D_FF), F32)],
        compiler_params=pltpu.CompilerParams(vmem_limit_bytes=VMEM_LIMIT),
        name="sample_out",
    )(x, ya, yb, sga, sgb, wba, wbb, wout, n2, wu, cw, cb, wd, h0, h1)


def _chunk_cols(a):
    lead = a.shape[:-1]
    a = a.reshape(lead + (2, N_FF_CHUNKS, FF_CHUNK))
    return jnp.moveaxis(a, (-3, -2), (0, 1))


def kernel(x_prompt, x_sample, cache_k, cache_v, state_conv, page_table, norm1_w, w_in, gmlp_vnorm_w, gmlp_ws, gmlp_bs, q_norm_w, k_norm_w, w_branch_a, w_branch_b, w_out, norm2_w, w_up, conv_w, conv_b, w_down):
    batch, seq, _ = x_prompt.shape
    n_seq, dec_seq, _ = x_sample.shape
    depth = w_in.shape[0]
    n_pages = page_table.shape[1]
    assert dec_seq == 1 and seq % ROW_TILE == 0 and ROW_TILE % MOBA_BLOCK == 0
    assert n_pages % PAGES_PER_BLOCK == 0
    n_tiles = batch * seq // ROW_TILE
    assert (n_seq * n_pages) % n_tiles == 0
    pages_per_tile = n_seq * n_pages // n_tiles

    pt_flat = page_table.reshape(-1)
    heads = lambda a: a.reshape(n_seq, N_HEADS, HEAD_DIM)

    xp = x_prompt.reshape(batch * seq, D_MODEL)
    xs = x_sample.reshape(n_seq, D_MODEL)
    row = lambda a: a.reshape(1, -1)
    k_p, v_p, k_s, v_s, gv_s, cv_p, cv_s = [], [], [], [], [], [], []
    for l in range(depth):
        w_in_b = w_in[l].astype(BF16)
        wba, wbb, wout = w_branch_a[l].astype(BF16), w_branch_b[l].astype(BF16), w_out[l].astype(BF16)
        wu_b, wd_b = w_up[l].astype(BF16), w_down[l].astype(BF16)
        n1, n2, vn, qn, kn = row(norm1_w[l]), row(norm2_w[l]), row(gmlp_vnorm_w[l]), row(q_norm_w[l]), row(k_norm_w[l])

        ya, q, k, vv, kb, vt, ksum, sga, sgb = _prompt_in(xp, n1, w_in_b, vn, qn, kn, gmlp_ws[l], gmlp_bs[l].T)
        yb = _prompt_attn(q, kb, vt, ksum, batch, seq)
        xp = _merge(xp, ya, yb, sga, sgb, wba, wbb, wout, ROW_TILE)
        xp, st, bsum = _prompt_ffn(xp, n2, _chunk_cols(wu_b), _chunk_cols(conv_w[l]),
                                   _chunk_cols(conv_b[l].reshape(1, -1)),
                                   wd_b.reshape(N_FF_CHUNKS, FF_CHUNK, D_MODEL), batch, seq,
                                   pt_flat, cache_k, l, 0, pages_per_tile)
        bsum = bsum.reshape(n_seq, n_pages // PAGES_PER_BLOCK, N_HEADS, HEAD_DIM)
        k_p.append(k.reshape(batch, seq, N_HEADS, HEAD_DIM))
        v_p.append(vv.reshape(batch, seq, N_HEADS, HEAD_DIM))
        cv_p.append(jnp.transpose(st, (0, 3, 1, 2, 4)).reshape(batch, CONV_W - 1, 2 * D_FF))

        wd0 = jnp.repeat(gmlp_ws[l][:, 0, 0], GROUP_DIM).reshape(1, -1)
        b0 = jnp.repeat(gmlp_bs[l][:, 0], GROUP_DIM).reshape(1, -1)
        ya, q, k, vv, gv, sga, sgb = _sample_in(xs, n1, w_in_b, vn, qn, kn, wd0, b0)
        idx = _sample_gate(heads(q), heads(k), bsum)
        idx_flat = idx[:, :, :MOBA_TOPK].reshape(-1)
        yb = _sample_attn(heads(q), heads(k), heads(vv), cache_k, cache_v, pt_flat, idx_flat, l, n_pages)
        yb = yb.reshape(n_seq, D_MODEL)
        hist = state_conv[l]
        xs, up = _sample_out(xs, ya, yb, sga, sgb, wba, wbb, wout, n2, wu_b, conv_w[l],
                             conv_b[l].reshape(1, -1), wd_b, hist[:, 0, :], hist[:, 1, :])
        k_s.append(k.reshape(n_seq, 1, N_HEADS, HEAD_DIM))
        v_s.append(vv.reshape(n_seq, 1, N_HEADS, HEAD_DIM))
        gv_s.append(gv.reshape(n_seq, 1, D_MODEL))
        cv_s.append(jnp.stack([hist[:, 1, :], up], axis=1))

    return (xp.reshape(batch, seq, D_MODEL), xs.reshape(n_seq, 1, D_MODEL),
            jnp.stack(k_p), jnp.stack(v_p), jnp.stack(k_s), jnp.stack(v_s),
            jnp.stack(gv_s), jnp.stack(cv_p), jnp.stack(cv_s))
```

```python
import functools

import jax
import jax.numpy as jnp
from jax import lax
from jax.experimental import pallas as pl
from jax.experimental.pallas import tpu as pltpu

F32 = jnp.float32
BF16 = jnp.bfloat16

D_MODEL = 1024
N_HEADS = 8
HEAD_DIM = 128
GMLP_GROUPS = 8
GROUP_DIM = 128
CHUNK = 128
MOBA_BLOCK = 256
MOBA_TOPK = 3
D_FF = 2816
CONV_W = 3
EPS = 1e-6
PAGE_SIZE = 128
N_SEG = 7
FF_CHUNK = 256
N_FF_CHUNKS = D_FF // FF_CHUNK
PAGE_SUB_BATCH = 8
PAGE_RING_SLOTS = 4
SUM_CHAINS = 8
VT_ROWS = HEAD_DIM + 16
PAGES_PER_BLOCK = MOBA_BLOCK // PAGE_SIZE
N_SEL_PAGES = MOBA_TOPK * PAGES_PER_BLOCK
ROW_TILE = 512
MERGE_ROW_TILE = 1024
ATTN_HEADS = 4
LOG2_E = 1.4426950408889634
VMEM_LIMIT = 60 * 1024 * 1024


def _rms(x, w):
    ms = jnp.mean(x * x, axis=-1, keepdims=True)
    return x * lax.rsqrt(ms + EPS) * w


def _const_spec(shape):
    nd = len(shape)
    return pl.BlockSpec(shape, lambda *_: (0,) * nd, pipeline_mode=pl.Buffered(1))


def _params(*sem):
    return pltpu.CompilerParams(dimension_semantics=sem, vmem_limit_bytes=VMEM_LIMIT)


def _top3_select(gate, pos, n_valid):
    neg = jnp.float32(-jnp.inf)
    gate = jnp.where(pos < n_valid, gate, neg)
    picks = []
    for _ in range(MOBA_TOPK):
        m = jnp.max(gate, axis=0, keepdims=True)
        first = jnp.min(jnp.where(gate == m, pos, jnp.float32(2 ** 30)), axis=0, keepdims=True)
        picks.append(first)
        gate = jnp.where(pos == first, neg, gate)
    return picks


class _PageSums:
    def __init__(self, pt_ref, cache_ref, out_ref, buf, sem, *, layer, first_page, pages_per_step):
        self.pt_ref, self.cache_ref, self.out_ref, self.buf, self.sem = pt_ref, cache_ref, out_ref, buf, sem
        self.layer, self.first_page = layer, first_page
        self.slots, self.sub = buf.shape[0], buf.shape[1]
        self.n_sub = pages_per_step // self.sub
        assert pages_per_step % self.sub == 0 and self.n_sub % self.slots == 0
        assert self.sub % PAGES_PER_BLOCK == 0
        self.step = pl.program_id(0)
        self.n_steps = pl.num_programs(0)

    def _copies(self, step, s):
        slot = s % self.slots
        base = self.first_page + (step * self.n_sub + s) * self.sub
        return [pltpu.make_async_copy(self.cache_ref.at[self.layer, self.pt_ref[base + p]],
                                      self.buf.at[slot, p], self.sem.at[slot]) for p in range(self.sub)]

    def _start(self, step, s):
        for cp in self._copies(step, s):
            cp.start()

    def prime(self):
        @pl.when(self.step == 0)
        def _():
            for s in range(self.slots):
                self._start(0, s)

    def _wait(self, s):
        for cp in self._copies(self.step, s):
            cp.wait()

    def _reduce(self, s):
        slot = s % self.slots
        per_sub = self.sub // PAGES_PER_BLOCK
        lanes_of = (PAGE_SIZE // SUM_CHAINS, SUM_CHAINS, N_HEADS, HEAD_DIM)
        for r in range(per_sub):
            part = jnp.sum(self.buf[slot, PAGES_PER_BLOCK * r].reshape(lanes_of), axis=0)
            for p in range(1, PAGES_PER_BLOCK):
                part = part + jnp.sum(self.buf[slot, PAGES_PER_BLOCK * r + p].reshape(lanes_of), axis=0)
            self.out_ref[s * per_sub + r] = jnp.sum(part, axis=0)

    def _refill(self, s):
        if s + self.slots < self.n_sub:
            self._start(self.step, s + self.slots)
        else:
            @pl.when(self.step + 1 < self.n_steps)
            def _():
                self._start(self.step + 1, s + self.slots - self.n_sub)

    def _at(self, k, n_points):
        return [s for s in range(self.n_sub) if s * n_points // self.n_sub == k]

    def wait_point(self, k, n_points):
        for s in self._at(k, n_points):
            self._wait(s)

    def reduce_point(self, k, n_points):
        for s in self._at(k, n_points):
            self._reduce(s)

    def after_point(self, k, n_points):
        for s in self._at(k, n_points):
            self._refill(s)


def _prompt_in_kernel(x_ref, n1_ref, w_ref, vn_ref, qn_ref, kn_ref, ws_ref, bst_ref,
                      ya_ref, q_ref, k_ref, vv_ref, kb_ref, vt_ref, ksum_ref, sga_ref, sgb_ref):
    tm = x_ref.shape[0]
    h = _rms(x_ref[...], n1_ref[...]).astype(BF16)

    def seg(s):
        return jnp.dot(h, w_ref[:, s * D_MODEL:(s + 1) * D_MODEL], preferred_element_type=F32)

    u = jax.nn.gelu(seg(0))
    v = _rms(jax.nn.gelu(seg(1)), vn_ref[...]).astype(BF16)
    row = lax.broadcasted_iota(jnp.int32, (CHUNK, CHUNK), 0)
    col = lax.broadcasted_iota(jnp.int32, (CHUNK, CHUNK), 1)
    for g in range(GMLP_GROUPS):
        wg = jnp.where(row >= col, ws_ref[g], 0.0).astype(BF16)
        bg = bst_ref[:, g:g + 1]
        cs = slice(g * GROUP_DIM, (g + 1) * GROUP_DIM)
        for c in range(tm // CHUNK):
            rs = slice(c * CHUNK, (c + 1) * CHUNK)
            mixed = jnp.dot(wg, v[rs, cs], preferred_element_type=F32) + bg
            ya_ref[rs, cs] = (u[rs, cs] * mixed).astype(BF16)

    aq = seg(2)
    for hd in range(N_HEADS):
        cs = slice(hd * HEAD_DIM, (hd + 1) * HEAD_DIM)
        q_ref[:, cs] = _rms(aq[:, cs], qn_ref[...])
    ak = seg(3)
    for hd in range(N_HEADS):
        cs = slice(hd * HEAD_DIM, (hd + 1) * HEAD_DIM)
        kh = _rms(ak[:, cs], kn_ref[...])
        k_ref[:, cs] = kh
        kb_ref[:, cs] = kh.astype(BF16)
    for blk in range(tm // MOBA_BLOCK):
        rs = slice(blk * MOBA_BLOCK, (blk + 1) * MOBA_BLOCK)
        ksum_ref[0, blk:blk + 1, :] = jnp.sum(k_ref[rs, :], axis=0, keepdims=True)
    av = seg(4)
    vv_ref[...] = av
    ones = jnp.ones((VT_ROWS - HEAD_DIM, MOBA_BLOCK), BF16)
    for blk in range(tm // MOBA_BLOCK):
        rs = slice(blk * MOBA_BLOCK, (blk + 1) * MOBA_BLOCK)
        for hd in range(N_HEADS):
            vt_ref[blk, hd, 0:HEAD_DIM, :] = av[rs, hd * HEAD_DIM:(hd + 1) * HEAD_DIM].T.astype(BF16)
            vt_ref[blk, hd, HEAD_DIM:VT_ROWS, :] = ones
    sga_ref[...] = jax.nn.sigmoid(seg(5))
    sgb_ref[...] = jax.nn.sigmoid(seg(6))


def _prompt_in(x, n1, w_in, vn, qn, kn, ws, bst):
    n = x.shape[0]
    tm = ROW_TILE
    nt = n // tm
    row_spec = pl.BlockSpec((tm, D_MODEL), lambda i: (i, 0))
    f32_out = jax.ShapeDtypeStruct((n, D_MODEL), F32)
    bf_out = jax.ShapeDtypeStruct((n, D_MODEL), BF16)
    return pl.pallas_call(
        _prompt_in_kernel,
        grid=(nt,),
        in_specs=[row_spec, _const_spec((1, D_MODEL)), _const_spec((D_MODEL, N_SEG * D_MODEL)),
                  _const_spec((1, D_MODEL)), _const_spec((1, HEAD_DIM)), _const_spec((1, HEAD_DIM)),
                  _const_spec((GMLP_GROUPS, CHUNK, CHUNK)), _const_spec((CHUNK, GMLP_GROUPS))],
        out_specs=[row_spec, row_spec, row_spec, row_spec, row_spec,
                   pl.BlockSpec((tm // MOBA_BLOCK, N_HEADS, VT_ROWS, MOBA_BLOCK), lambda i: (i, 0, 0, 0)),
                   pl.BlockSpec((1, tm // MOBA_BLOCK, D_MODEL), lambda i: (i, 0, 0)),
                   row_spec, row_spec],
        out_shape=[bf_out, f32_out, f32_out, f32_out, bf_out,
                   jax.ShapeDtypeStruct((n // MOBA_BLOCK, N_HEADS, VT_ROWS, MOBA_BLOCK), BF16),
                   jax.ShapeDtypeStruct((nt, tm // MOBA_BLOCK, D_MODEL), F32),
                   f32_out, f32_out],
        compiler_params=_params("parallel"),
        name="prompt_in",
    )(x, n1, w_in, vn, qn, kn, ws, bst)


def _prompt_attn_kernel(q_ref, kb_ref, vt_ref, ksum_ref, o_ref,
                        qt_scr, sel_scr, m_scr, acc_scr, p_scr, s_even, s_odd):
    i = pl.program_id(2)
    nb = ksum_ref.shape[1]
    blk = MOBA_BLOCK
    c = HEAD_DIM ** -0.5 * LOG2_E
    neg = jnp.float32(-jnp.inf)
    heads = range(ATTN_HEADS)
    cols = [slice(g * HEAD_DIM, (g + 1) * HEAD_DIM) for g in heads]
    pos = lax.broadcasted_iota(jnp.int32, (nb, blk), 0).astype(F32)
    n_past = i.astype(F32)

    def scores(j, dst):
        rows = pl.ds(pl.multiple_of(j * blk, blk), blk)
        for g in heads:
            dst[g] = jnp.dot(kb_ref[rows, cols[g]], qt_scr[g], preferred_element_type=F32)

    def pv(j, alphas):
        for g in heads:
            upd = jnp.dot(vt_ref[j, g], p_scr[g], preferred_element_type=F32)
            acc_scr[g] = upd if alphas is None else alphas[g] * acc_scr[g] + upd

    for g in heads:
        qt = q_ref[:, cols[g]].T
        qt_scr[g] = (qt * c).astype(BF16)
        kbar = ksum_ref[0, :, cols[g]] * (1.0 / blk)
        gate = jnp.dot(kbar, qt, precision=lax.Precision.HIGHEST, preferred_element_type=F32)
        sel = jnp.zeros((nb, blk), F32)
        for first in _top3_select(gate, pos, n_past):
            sel = jnp.where((pos == first) & (pos < n_past), 1.0, sel)
        sel_scr[g] = sel

    scores(i, s_odd)
    scores(0, s_even)
    key = lax.broadcasted_iota(jnp.int32, (blk, blk), 0)
    qry = lax.broadcasted_iota(jnp.int32, (blk, blk), 1)
    for g in heads:
        s = jnp.where(key <= qry, s_odd[g], neg)
        m = jnp.max(s, axis=0, keepdims=True)
        p = jnp.exp2(s - m)
        m_scr[g] = m
        p_scr[g] = p.astype(BF16)
    pv(i, None)

    def past_block(j, src, dst):
        if dst is not None:
            scores(j + 1, dst)
        alphas = []
        for g in heads:
            s = src[g]
            chosen = sel_scr[g, pl.ds(j, 1), :] > 0.5
            m_old = m_scr[g]
            m_new = jnp.where(chosen, jnp.maximum(m_old, jnp.max(s, axis=0, keepdims=True)), m_old)
            p = jnp.exp2(s - jnp.where(chosen, m_new, jnp.float32(jnp.inf)))
            alpha = jnp.exp2(m_old - m_new)
            m_scr[g] = m_new
            p_scr[g] = p.astype(BF16)
            alphas.append(alpha)
        pv(j, alphas)

    def two_blocks(t, _):
        past_block(2 * t, s_even, s_odd)
        past_block(2 * t + 1, s_odd, s_even)
        return 0

    lax.fori_loop(0, i // 2, two_blocks, 0)

    @pl.when(i % 2 == 1)
    def _():
        past_block(i - 1, s_even, None)

    for g in heads:
        out = acc_scr[g, 0:HEAD_DIM, :] / acc_scr[g, HEAD_DIM:HEAD_DIM + 1, :]
        o_ref[:, cols[g]] = out.T.astype(o_ref.dtype)


def _prompt_attn(q, kb, vt, ksum, batch, seq):
    n = q.shape[0]
    nb = seq // MOBA_BLOCK
    gw = ATTN_HEADS * HEAD_DIM
    ksum = ksum.reshape(batch, nb, D_MODEL)
    return pl.pallas_call(
        _prompt_attn_kernel,
        grid=(batch, N_HEADS // ATTN_HEADS, nb),
        in_specs=[pl.BlockSpec((MOBA_BLOCK, gw), lambda b, h, i: (b * nb + i, h)),
                  pl.BlockSpec((seq, gw), lambda b, h, i: (b, h)),
                  pl.BlockSpec((nb, ATTN_HEADS, VT_ROWS, MOBA_BLOCK), lambda b, h, i: (b, h, 0, 0)),
                  pl.BlockSpec((1, nb, gw), lambda b, h, i: (b, 0, h))],
        out_specs=pl.BlockSpec((MOBA_BLOCK, gw), lambda b, h, i: (b * nb + i, h)),
        out_shape=jax.ShapeDtypeStruct((n, D_MODEL), BF16),
        scratch_shapes=[pltpu.VMEM((ATTN_HEADS, HEAD_DIM, MOBA_BLOCK), BF16),
                        pltpu.VMEM((ATTN_HEADS, nb, MOBA_BLOCK), F32),
                        pltpu.VMEM((ATTN_HEADS, 1, MOBA_BLOCK), F32),
                        pltpu.VMEM((ATTN_HEADS, VT_ROWS, MOBA_BLOCK), F32),
                        pltpu.VMEM((ATTN_HEADS, MOBA_BLOCK, MOBA_BLOCK), BF16),
                        pltpu.VMEM((ATTN_HEADS, MOBA_BLOCK, MOBA_BLOCK), F32),
                        pltpu.VMEM((ATTN_HEADS, MOBA_BLOCK, MOBA_BLOCK), F32)],
        compiler_params=_params("parallel", "parallel", "arbitrary"),
        name="prompt_attn",
    )(q, kb, vt, ksum)


def _merge_kernel(x_ref, ya_ref, yb_ref, sga_ref, sgb_ref, wba_ref, wbb_ref, wout_ref, o_ref):
    a = jnp.dot(ya_ref[...], wba_ref[...], preferred_element_type=F32)
    b = jnp.dot(yb_ref[...], wbb_ref[...], preferred_element_type=F32)
    mix = (sga_ref[...] * a + sgb_ref[...] * b).astype(BF16)
    o_ref[...] = x_ref[...] + jnp.dot(mix, wout_ref[...], preferred_element_type=F32)


def _merge(x, ya, yb, sga, sgb, wba, wbb, wout, tm):
    n = x.shape[0]
    row_spec = pl.BlockSpec((tm, D_MODEL), lambda i: (i, 0))
    w_spec = _const_spec((D_MODEL, D_MODEL))
    return pl.pallas_call(
        _merge_kernel,
        grid=(n // tm,),
        in_specs=[row_spec] * 5 + [w_spec] * 3,
        out_specs=row_spec,
        out_shape=jax.ShapeDtypeStruct((n, D_MODEL), F32),
        compiler_params=_params("parallel"),
        name="merge",
    )(x, ya, yb, sga, sgb, wba, wbb, wout)


def _prompt_ffn_kernel(pt_ref, x_ref, n2_ref, wu_ref, cw_ref, cb_ref, wd_ref, cache_ref,
                       o_ref, st_ref, bsum_ref, h_scr, carry_scr, page_buf, page_sem,
                       *, tiles_per_seq, layer, first_page, pages_per_step):
    tm = x_ref.shape[0]
    pages = _PageSums(pt_ref, cache_ref, bsum_ref, page_buf, page_sem, layer=layer,
                      first_page=first_page, pages_per_step=pages_per_step)
    pages.prime()

    @pl.when(pl.program_id(0) % tiles_per_seq == 0)
    def _():
        carry_scr[...] = jnp.zeros_like(carry_scr)

    x = x_ref[...]
    h_scr[...] = _rms(x, n2_ref[...]).astype(BF16)
    row = lax.broadcasted_iota(jnp.int32, (8, FF_CHUNK), 0)

    def up_proj(part, c):
        if c >= N_FF_CHUNKS:
            return None
        return jnp.dot(h_scr[...], wu_ref[part, c], preferred_element_type=F32)

    def conv(up, part, c):
        prev = carry_scr[part, c]
        r1, r2 = pltpu.roll(up, 1, 0), pltpu.roll(up, 2, 0)
        top1 = jnp.where(row == 0, prev[1:2, :], r1[0:8, :])
        top2 = jnp.where(row == 0, prev[0:1, :], jnp.where(row == 1, prev[1:2, :], r2[0:8, :]))
        u1 = jnp.concatenate([top1, r1[8:, :]], axis=0)
        u2 = jnp.concatenate([top2, r2[8:, :]], axis=0)
        carry_scr[part, c] = up[tm - 2:tm, :]
        cw = cw_ref[part, c]
        return cb_ref[part, c] + cw[0:1, :] * u2 + cw[1:2, :] * u1 + cw[2:3, :] * up

    acc = x
    up_g, up_z = up_proj(0, 0), up_proj(1, 0)
    for c in range(N_FF_CHUNKS):
        pages.wait_point(c, N_FF_CHUNKS)
        next_g = up_proj(0, c + 1)
        gate = jax.nn.gelu(conv(up_g, 0, c))
        next_z = up_proj(1, c + 1)
        act = (gate * conv(up_z, 1, c)).astype(BF16)
        acc = acc + jnp.dot(act, wd_ref[c], preferred_element_type=F32)
        pages.reduce_point(c, N_FF_CHUNKS)
        pages.after_point(c, N_FF_CHUNKS)
        up_g, up_z = next_g, next_z
    o_ref[...] = acc
    st_ref[0] = carry_scr[...]


def _page_sum_specs(n_steps, pages_per_step):
    blocks = pages_per_step // PAGES_PER_BLOCK
    out_spec = pl.BlockSpec((blocks, N_HEADS, HEAD_DIM), lambda i, pt: (i, 0, 0))
    out_shape = jax.ShapeDtypeStruct((n_steps * blocks, N_HEADS, HEAD_DIM), F32)
    scratch = [pltpu.VMEM((PAGE_RING_SLOTS, PAGE_SUB_BATCH, PAGE_SIZE, N_HEADS, HEAD_DIM), F32),
               pltpu.SemaphoreType.DMA((PAGE_RING_SLOTS,))]
    return out_spec, out_shape, scratch


def _prompt_ffn(x, n2, wu_c, cw_c, cb_c, wd_c, batch, seq, page_table_flat, cache_k, layer, first_page,
                pages_per_step):
    n = x.shape[0]
    tm = ROW_TILE
    tiles_per_seq = seq // tm
    n_steps = n // tm
    row_spec = pl.BlockSpec((tm, D_MODEL), lambda i, pt: (i, 0))
    st_shape = (2, N_FF_CHUNKS, CONV_W - 1, FF_CHUNK)
    bsum_spec, bsum_shape, page_scratch = _page_sum_specs(n_steps, pages_per_step)
    return pl.pallas_call(
        functools.partial(_prompt_ffn_kernel, tiles_per_seq=tiles_per_seq, layer=layer,
                          first_page=first_page, pages_per_step=pages_per_step),
        grid_spec=pltpu.PrefetchScalarGridSpec(
            num_scalar_prefetch=1,
            grid=(n_steps,),
            in_specs=[row_spec, _const_spec((1, D_MODEL)),
                      _const_spec((2, N_FF_CHUNKS, D_MODEL, FF_CHUNK)),
                      _const_spec((2, N_FF_CHUNKS, CONV_W, FF_CHUNK)),
                      _const_spec((2, N_FF_CHUNKS, 1, FF_CHUNK)),
                      _const_spec((N_FF_CHUNKS, FF_CHUNK, D_MODEL)),
                      pl.BlockSpec(memory_space=pl.ANY)],
            out_specs=[row_spec,
                       pl.BlockSpec((1,) + st_shape, lambda i, pt: (i // tiles_per_seq, 0, 0, 0, 0)),
                       bsum_spec],
            scratch_shapes=[pltpu.VMEM((tm, D_MODEL), BF16), pltpu.VMEM(st_shape, F32)] + page_scratch,
        ),
        out_shape=[jax.ShapeDtypeStruct((n, D_MODEL), F32),
                   jax.ShapeDtypeStruct((batch,) + st_shape, F32),
                   bsum_shape],
        compiler_params=_params("arbitrary"),
        name="prompt_ffn",
    )(page_table_flat, x, n2, wu_c, cw_c, cb_c, wd_c, cache_k)


def _sample_in_kernel(x_ref, n1_ref, w_ref, vn_ref, qn_ref, kn_ref, wd0_ref, b0_ref,
                      ya_ref, q_ref, k_ref, vv_ref, gv_ref, sga_ref, sgb_ref):
    h = _rms(x_ref[...], n1_ref[...]).astype(BF16)

    def seg(s):
        return jnp.dot(h, w_ref[:, s * D_MODEL:(s + 1) * D_MODEL], preferred_element_type=F32)

    u = jax.nn.gelu(seg(0))
    v = _rms(jax.nn.gelu(seg(1)), vn_ref[...])
    gv_ref[...] = v
    ya_ref[...] = (u * (wd0_ref[...] * v + b0_ref[...])).astype(BF16)
    aq = seg(2)
    ak = seg(3)
    for hd in range(N_HEADS):
        cs = slice(hd * HEAD_DIM, (hd + 1) * HEAD_DIM)
        q_ref[:, cs] = _rms(aq[:, cs], qn_ref[...])
        k_ref[:, cs] = _rms(ak[:, cs], kn_ref[...])
    vv_ref[...] = seg(4)
    sga_ref[...] = jax.nn.sigmoid(seg(5))
    sgb_ref[...] = jax.nn.sigmoid(seg(6))


def _sample_in(x, n1, w_in, vn, qn, kn, wd0, b0):
    n = x.shape[0]
    f32_out = jax.ShapeDtypeStruct((n, D_MODEL), F32)
    return pl.pallas_call(
        _sample_in_kernel,
        out_shape=[jax.ShapeDtypeStruct((n, D_MODEL), BF16)] + [f32_out] * 6,
        compiler_params=pltpu.CompilerParams(vmem_limit_bytes=VMEM_LIMIT),
        name="sample_in",
    )(x, n1, w_in, vn, qn, kn, wd0, b0)


def _sample_gate_kernel(q_ref, k_ref, bsum_ref, idx_ref, *, n_past):
    q = q_ref[0]
    inv = 1.0 / MOBA_BLOCK
    past = (bsum_ref[0] * inv) * q
    own = (k_ref[0] * inv) * q
    prod = jnp.concatenate([past, own[None]], axis=0)
    ones = jnp.ones((HEAD_DIM, HEAD_DIM), F32)
    gate = jnp.dot(prod.reshape((n_past + 1) * N_HEADS, HEAD_DIM), ones,
                   precision=lax.Precision.HIGHEST, preferred_element_type=F32)
    gate = gate.reshape(n_past + 1, N_HEADS, HEAD_DIM)
    pos = lax.broadcasted_iota(jnp.int32, gate.shape, 0).astype(F32)
    lane = lax.broadcasted_iota(jnp.int32, (N_HEADS, HEAD_DIM), 1)
    out = jnp.zeros((N_HEADS, HEAD_DIM), jnp.int32)
    for r, first in enumerate(_top3_select(gate, pos, jnp.float32(n_past))):
        out = jnp.where(lane == r, first[0].astype(jnp.int32), out)
    idx_ref[0] = out


def _sample_gate(q, k, bsum):
    n = q.shape[0]
    n_past = bsum.shape[1]
    vec_spec = pl.BlockSpec((1, N_HEADS, HEAD_DIM), lambda b: (b, 0, 0))
    return pl.pallas_call(
        functools.partial(_sample_gate_kernel, n_past=n_past),
        grid=(n,),
        in_specs=[vec_spec, vec_spec,
                  pl.BlockSpec((1, n_past, N_HEADS, HEAD_DIM), lambda b: (b, 0, 0, 0))],
        out_specs=vec_spec,
        out_shape=jax.ShapeDtypeStruct((n, N_HEADS, HEAD_DIM), jnp.int32),
        compiler_params=_params("parallel"),
        name="sample_gate",
    )(q, k, bsum)


def _sample_attn_kernel(pt_ref, idx_ref, q_ref, kn_ref, vn_ref, ck_ref, cv_ref, o_ref,
                        kbuf, vbuf, sem, *, layer, n_pages):
    b = pl.program_id(0)
    n_seq = pl.num_programs(0)

    def page_copies(seq, slot):
        out = []
        for h in range(N_HEADS):
            for j in range(N_SEL_PAGES):
                r, p = divmod(j, PAGES_PER_BLOCK)
                blk = idx_ref[(seq * N_HEADS + h) * MOBA_TOPK + r]
                phys = pt_ref[seq * n_pages + blk * PAGES_PER_BLOCK + p]
                rows = pl.ds(j * PAGE_SIZE, PAGE_SIZE)
                out.append(pltpu.make_async_copy(ck_ref.at[layer, phys, :, h, :],
                                                 kbuf.at[slot, h, rows, :], sem.at[slot]))
                out.append(pltpu.make_async_copy(cv_ref.at[layer, phys, :, h, :],
                                                 vbuf.at[slot, h, rows, :], sem.at[slot]))
        return out

    @pl.when(b == 0)
    def _():
        for cp in page_copies(0, 0):
            cp.start()

    @pl.when(b + 1 < n_seq)
    def _():
        for cp in page_copies(b + 1, (b + 1) % 2):
            cp.start()

    slot = b % 2
    for cp in page_copies(b, slot):
        cp.wait()

    scale = HEAD_DIM ** -0.5
    for h in range(N_HEADS):
        q = q_ref[0, h:h + 1, :]
        s = jnp.sum(kbuf[slot, h] * q, axis=-1, keepdims=True) * scale
        s_own = jnp.sum(kn_ref[0, h:h + 1, :] * q, axis=-1, keepdims=True) * scale
        m = jnp.maximum(jnp.max(s, axis=0, keepdims=True), s_own)
        p = jnp.exp(s - m)
        p_own = jnp.exp(s_own - m)
        l = jnp.sum(p, axis=0, keepdims=True) + p_own
        acc = jnp.sum(p * vbuf[slot, h], axis=0, keepdims=True) + p_own * vn_ref[0, h:h + 1, :]
        o_ref[0, h:h + 1, :] = (acc / l).astype(o_ref.dtype)


def _sample_attn(q, k_new, v_new, cache_k, cache_v, page_table_flat, idx_flat, layer, n_pages):
    n = q.shape[0]
    vec_spec = pl.BlockSpec((1, N_HEADS, HEAD_DIM), lambda b, pt, ix: (b, 0, 0))
    any_spec = pl.BlockSpec(memory_space=pl.ANY)
    buf = pltpu.VMEM((2, N_HEADS, N_SEL_PAGES * PAGE_SIZE, HEAD_DIM), F32)
    return pl.pallas_call(
        functools.partial(_sample_attn_kernel, layer=layer, n_pages=n_pages),
        grid_spec=pltpu.PrefetchScalarGridSpec(
            num_scalar_prefetch=2,
            grid=(n,),
            in_specs=[vec_spec, vec_spec, vec_spec, any_spec, any_spec],
            out_specs=vec_spec,
            scratch_shapes=[buf, buf, pltpu.SemaphoreType.DMA((2,))],
        ),
        out_shape=jax.ShapeDtypeStruct((n, N_HEADS, HEAD_DIM), BF16),
        compiler_params=_params("arbitrary"),
        name="sample_attn",
    )(page_table_flat, idx_flat, q, k_new, v_new, cache_k, cache_v)


def _sample_out_kernel(x_ref, ya_ref, yb_ref, sga_ref, sgb_ref, wba_ref, wbb_ref, wout_ref,
                       n2_ref, wu_ref, cw_ref, cb_ref, wd_ref, h0_ref, h1_ref, o_ref, up_ref):
    a = jnp.dot(ya_ref[...], wba_ref[...], preferred_element_type=F32)
    b = jnp.dot(yb_ref[...], wbb_ref[...], preferred_element_type=F32)
    mix = (sga_ref[...] * a + sgb_ref[...] * b).astype(BF16)
    x = x_ref[...] + jnp.dot(mix, wout_ref[...], preferred_element_type=F32)
    h = _rms(x, n2_ref[...]).astype(BF16)
    up = jnp.dot(h, wu_ref[...], preferred_element_type=F32)
    up_ref[...] = up
    c = cb_ref[...] + cw_ref[0:1, :] * h0_ref[...] + cw_ref[1:2, :] * h1_ref[...] + cw_ref[2:3, :] * up
    act = (jax.nn.gelu(c[:, :D_FF]) * c[:, D_FF:]).astype(BF16)
    o_ref[...] = x + jnp.dot(act, wd_ref[...], preferred_element_type=F32)


def _sample_out(x, ya, yb, sga, sgb, wba, wbb, wout, n2, wu, cw, cb, wd, h0, h1):
    n = x.shape[0]
    return pl.pallas_call(
        _sample_out_kernel,
        out_shape=[jax.ShapeDtypeStruct((n, D_MODEL), F32), jax.ShapeDtypeStruct((n, 2 * D_FF), F32)],
        compiler_params=pltpu.CompilerParams(vmem_limit_bytes=VMEM_LIMIT),
        name="sample_out",
    )(x, ya, yb, sga, sgb, wba, wbb, wout, n2, wu, cw, cb, wd, h0, h1)


def _chunk_cols(a):
    lead = a.shape[:-1]
    a = a.reshape(lead + (2, N_FF_CHUNKS, FF_CHUNK))
    return jnp.moveaxis(a, (-3, -2), (0, 1))


def kernel(x_prompt, x_sample, cache_k, cache_v, state_conv, page_table, norm1_w, w_in, gmlp_vnorm_w, gmlp_ws, gmlp_bs, q_norm_w, k_norm_w, w_branch_a, w_branch_b, w_out, norm2_w, w_up, conv_w, conv_b, w_down):
    batch, seq, _ = x_prompt.shape
    n_seq, dec_seq, _ = x_sample.shape
    depth = w_in.shape[0]
    n_pages = page_table.shape[1]
    assert dec_seq == 1 and seq % ROW_TILE == 0 and ROW_TILE % MOBA_BLOCK == 0
    assert n_pages % PAGES_PER_BLOCK == 0
    n_tiles = batch * seq // ROW_TILE
    assert (n_seq * n_pages) % n_tiles == 0
    pages_per_tile = n_seq * n_pages // n_tiles

    pt_flat = page_table.reshape(-1)
    heads = lambda a: a.reshape(n_seq, N_HEADS, HEAD_DIM)

    xp = x_prompt.reshape(batch * seq, D_MODEL)
    xs = x_sample.reshape(n_seq, D_MODEL)
    row = lambda a: a.reshape(1, -1)
    k_p, v_p, k_s, v_s, gv_s, cv_p, cv_s = [], [], [], [], [], [], []
    for l in range(depth):
        w_in_b = w_in[l].astype(BF16)
        wba, wbb, wout = w_branch_a[l].astype(BF16), w_branch_b[l].astype(BF16), w_out[l].astype(BF16)
        wu_b, wd_b = w_up[l].astype(BF16), w_down[l].astype(BF16)
        n1, n2, vn, qn, kn = row(norm1_w[l]), row(norm2_w[l]), row(gmlp_vnorm_w[l]), row(q_norm_w[l]), row(k_norm_w[l])

        ya, q, k, vv, kb, vt, ksum, sga, sgb = _prompt_in(xp, n1, w_in_b, vn, qn, kn, gmlp_ws[l], gmlp_bs[l].T)
        yb = _prompt_attn(q, kb, vt, ksum, batch, seq)
        xp = _merge(xp, ya, yb, sga, sgb, wba, wbb, wout, MERGE_ROW_TILE)
        xp, st, bsum = _prompt_ffn(xp, n2, _chunk_cols(wu_b), _chunk_cols(conv_w[l]),
                                   _chunk_cols(conv_b[l].reshape(1, -1)),
                                   wd_b.reshape(N_FF_CHUNKS, FF_CHUNK, D_MODEL), batch, seq,
                                   pt_flat, cache_k, l, 0, pages_per_tile)
        bsum = bsum.reshape(n_seq, n_pages // PAGES_PER_BLOCK, N_HEADS, HEAD_DIM)
        k_p.append(k.reshape(batch, seq, N_HEADS, HEAD_DIM))
        v_p.append(vv.reshape(batch, seq, N_HEADS, HEAD_DIM))
        cv_p.append(jnp.transpose(st, (0, 3, 1, 2, 4)).reshape(batch, CONV_W - 1, 2 * D_FF))

        wd0 = jnp.repeat(gmlp_ws[l][:, 0, 0], GROUP_DIM).reshape(1, -1)
        b0 = jnp.repeat(gmlp_bs[l][:, 0], GROUP_DIM).reshape(1, -1)
        ya, q, k, vv, gv, sga, sgb = _sample_in(xs, n1, w_in_b, vn, qn, kn, wd0, b0)
        idx = _sample_gate(heads(q), heads(k), bsum)
        idx_flat = idx[:, :, :MOBA_TOPK].reshape(-1)
        yb = _sample_attn(heads(q), heads(k), heads(vv), cache_k, cache_v, pt_flat, idx_flat, l, n_pages)
        yb = yb.reshape(n_seq, D_MODEL)
        hist = state_conv[l]
        xs, up = _sample_out(xs, ya, yb, sga, sgb, wba, wbb, wout, n2, wu_b, conv_w[l],
                             conv_b[l].reshape(1, -1), wd_b, hist[:, 0, :], hist[:, 1, :])
        k_s.append(k.reshape(n_seq, 1, N_HEADS, HEAD_DIM))
        v_s.append(vv.reshape(n_seq, 1, N_HEADS, HEAD_DIM))
        gv_s.append(gv.reshape(n_seq, 1, D_MODEL))
        cv_s.append(jnp.stack([hist[:, 1, :], up], axis=1))

    return (xp.reshape(batch, seq, D_MODEL), xs.reshape(n_seq, 1, D_MODEL),
            jnp.stack(k_p), jnp.stack(v_p), jnp.stack(k_s), jnp.stack(v_s),
            jnp.stack(gv_s), jnp.stack(cv_p), jnp.stack(cv_s))
```

```python
import functools

import jax
import jax.numpy as jnp
from jax import lax
from jax.experimental import pallas as pl
from jax.experimental.pallas import tpu as pltpu

F32 = jnp.float32
BF16 = jnp.bfloat16

D_MODEL = 1024
N_HEADS = 8
HEAD_DIM = 128
GMLP_GROUPS = 8
GROUP_DIM = 128
CHUNK = 128
MOBA_BLOCK = 256
MOBA_TOPK = 3
D_FF = 2816
CONV_W = 3
EPS = 1e-6
PAGE_SIZE = 128
N_SEG = 7
FF_CHUNK = 256
N_FF_CHUNKS = D_FF // FF_CHUNK
PAGE_SUB_BATCH = 8
PAGE_RING_SLOTS = 4
SUM_CHAINS = 8
VT_ROWS = HEAD_DIM + 16
PAGES_PER_BLOCK = MOBA_BLOCK // PAGE_SIZE
N_SEL_PAGES = MOBA_TOPK * PAGES_PER_BLOCK
ROW_TILE = 512
MERGE_ROW_TILE = 1024
ATTN_HEADS = 4
LOG2_E = 1.4426950408889634
VMEM_LIMIT = 60 * 1024 * 1024


def _rms(x, w):
    ms = jnp.mean(x * x, axis=-1, keepdims=True)
    return x * lax.rsqrt(ms + EPS) * w


def _const_spec(shape):
    nd = len(shape)
    return pl.BlockSpec(shape, lambda *_: (0,) * nd, pipeline_mode=pl.Buffered(1))


def _params(*sem):
    return pltpu.CompilerParams(dimension_semantics=sem, vmem_limit_bytes=VMEM_LIMIT)


def _top3_select(gate, pos, n_valid):
    neg = jnp.float32(-jnp.inf)
    gate = jnp.where(pos < n_valid, gate, neg)
    picks = []
    for _ in range(MOBA_TOPK):
        m = jnp.max(gate, axis=0, keepdims=True)
        first = jnp.min(jnp.where(gate == m, pos, jnp.float32(2 ** 30)), axis=0, keepdims=True)
        picks.append(first)
        gate = jnp.where(pos == first, neg, gate)
    return picks


class _PageSums:
    def __init__(self, pt_ref, cache_ref, out_ref, buf, sem, *, layer, first_page, pages_per_step):
        self.pt_ref, self.cache_ref, self.out_ref, self.buf, self.sem = pt_ref, cache_ref, out_ref, buf, sem
        self.layer, self.first_page = layer, first_page
        self.slots, self.sub = buf.shape[0], buf.shape[1]
        self.n_sub = pages_per_step // self.sub
        assert pages_per_step % self.sub == 0 and self.n_sub % self.slots == 0
        assert self.sub % PAGES_PER_BLOCK == 0
        self.step = pl.program_id(0)
        self.n_steps = pl.num_programs(0)

    def _copies(self, step, s):
        slot = s % self.slots
        base = self.first_page + (step * self.n_sub + s) * self.sub
        return [pltpu.make_async_copy(self.cache_ref.at[self.layer, self.pt_ref[base + p]],
                                      self.buf.at[slot, p], self.sem.at[slot]) for p in range(self.sub)]

    def _start(self, step, s):
        for cp in self._copies(step, s):
            cp.start()

    def prime(self):
        @pl.when(self.step == 0)
        def _():
            for s in range(self.slots):
                self._start(0, s)

    def _wait(self, s):
        for cp in self._copies(self.step, s):
            cp.wait()

    def _reduce(self, s):
        slot = s % self.slots
        per_sub = self.sub // PAGES_PER_BLOCK
        lanes_of = (PAGE_SIZE // SUM_CHAINS, SUM_CHAINS, N_HEADS, HEAD_DIM)
        for r in range(per_sub):
            part = jnp.sum(self.buf[slot, PAGES_PER_BLOCK * r].reshape(lanes_of), axis=0)
            for p in range(1, PAGES_PER_BLOCK):
                part = part + jnp.sum(self.buf[slot, PAGES_PER_BLOCK * r + p].reshape(lanes_of), axis=0)
            self.out_ref[s * per_sub + r] = jnp.sum(part, axis=0)

    def _refill(self, s):
        if s + self.slots < self.n_sub:
            self._start(self.step, s + self.slots)
        else:
            @pl.when(self.step + 1 < self.n_steps)
            def _():
                self._start(self.step + 1, s + self.slots - self.n_sub)

    def _at(self, k, n_points):
        return [s for s in range(self.n_sub) if s * n_points // self.n_sub == k]

    def wait_point(self, k, n_points):
        for s in self._at(k, n_points):
            self._wait(s)

    def reduce_point(self, k, n_points):
        for s in self._at(k, n_points):
            self._reduce(s)

    def after_point(self, k, n_points):
        for s in self._at(k, n_points):
            self._refill(s)


def _prompt_in_kernel(x_ref, n1_ref, w_ref, vn_ref, qn_ref, kn_ref, ws_ref, bst_ref,
                      ya_ref, q_ref, k_ref, vv_ref, kb_ref, vt_ref, ksum_ref, sga_ref, sgb_ref):
    tm = x_ref.shape[0]
    h = _rms(x_ref[...], n1_ref[...]).astype(BF16)

    def seg(s):
        return jnp.dot(h, w_ref[:, s * D_MODEL:(s + 1) * D_MODEL], preferred_element_type=F32)

    u = jax.nn.gelu(seg(0))
    v = _rms(jax.nn.gelu(seg(1)), vn_ref[...]).astype(BF16)
    row = lax.broadcasted_iota(jnp.int32, (CHUNK, CHUNK), 0)
    col = lax.broadcasted_iota(jnp.int32, (CHUNK, CHUNK), 1)
    for g in range(GMLP_GROUPS):
        wg = jnp.where(row >= col, ws_ref[g], 0.0).astype(BF16)
        bg = bst_ref[:, g:g + 1]
        cs = slice(g * GROUP_DIM, (g + 1) * GROUP_DIM)
        for c in range(tm // CHUNK):
            rs = slice(c * CHUNK, (c + 1) * CHUNK)
            mixed = jnp.dot(wg, v[rs, cs], preferred_element_type=F32) + bg
            ya_ref[rs, cs] = (u[rs, cs] * mixed).astype(BF16)

    aq = seg(2)
    for hd in range(N_HEADS):
        cs = slice(hd * HEAD_DIM, (hd + 1) * HEAD_DIM)
        q_ref[:, cs] = _rms(aq[:, cs], qn_ref[...])
    ak = seg(3)
    for hd in range(N_HEADS):
        cs = slice(hd * HEAD_DIM, (hd + 1) * HEAD_DIM)
        kh = _rms(ak[:, cs], kn_ref[...])
        k_ref[:, cs] = kh
        kb_ref[:, cs] = kh.astype(BF16)
    for blk in range(tm // MOBA_BLOCK):
        rs = slice(blk * MOBA_BLOCK, (blk + 1) * MOBA_BLOCK)
        ksum_ref[0, blk:blk + 1, :] = jnp.sum(k_ref[rs, :], axis=0, keepdims=True)
    av = seg(4)
    vv_ref[...] = av
    ones = jnp.ones((VT_ROWS - HEAD_DIM, MOBA_BLOCK), BF16)
    for blk in range(tm // MOBA_BLOCK):
        rs = slice(blk * MOBA_BLOCK, (blk + 1) * MOBA_BLOCK)
        for hd in range(N_HEADS):
            vt_ref[blk, hd, 0:HEAD_DIM, :] = av[rs, hd * HEAD_DIM:(hd + 1) * HEAD_DIM].T.astype(BF16)
            vt_ref[blk, hd, HEAD_DIM:VT_ROWS, :] = ones
    sga_ref[...] = jax.nn.sigmoid(seg(5)).astype(sga_ref.dtype)
    sgb_ref[...] = jax.nn.sigmoid(seg(6)).astype(sgb_ref.dtype)


def _prompt_in(x, n1, w_in, vn, qn, kn, ws, bst):
    n = x.shape[0]
    tm = ROW_TILE
    nt = n // tm
    row_spec = pl.BlockSpec((tm, D_MODEL), lambda i: (i, 0))
    f32_out = jax.ShapeDtypeStruct((n, D_MODEL), F32)
    bf_out = jax.ShapeDtypeStruct((n, D_MODEL), BF16)
    return pl.pallas_call(
        _prompt_in_kernel,
        grid=(nt,),
        in_specs=[row_spec, _const_spec((1, D_MODEL)), _const_spec((D_MODEL, N_SEG * D_MODEL)),
                  _const_spec((1, D_MODEL)), _const_spec((1, HEAD_DIM)), _const_spec((1, HEAD_DIM)),
                  _const_spec((GMLP_GROUPS, CHUNK, CHUNK)), _const_spec((CHUNK, GMLP_GROUPS))],
        out_specs=[row_spec, row_spec, row_spec, row_spec, row_spec,
                   pl.BlockSpec((tm // MOBA_BLOCK, N_HEADS, VT_ROWS, MOBA_BLOCK), lambda i: (i, 0, 0, 0)),
                   pl.BlockSpec((1, tm // MOBA_BLOCK, D_MODEL), lambda i: (i, 0, 0)),
                   row_spec, row_spec],
        out_shape=[bf_out, f32_out, f32_out, f32_out, bf_out,
                   jax.ShapeDtypeStruct((n // MOBA_BLOCK, N_HEADS, VT_ROWS, MOBA_BLOCK), BF16),
                   jax.ShapeDtypeStruct((nt, tm // MOBA_BLOCK, D_MODEL), F32),
                   bf_out, bf_out],
        compiler_params=_params("parallel"),
        name="prompt_in",
    )(x, n1, w_in, vn, qn, kn, ws, bst)


def _prompt_attn_kernel(q_ref, kb_ref, vt_ref, ksum_ref, o_ref,
                        qt_scr, sel_scr, m_scr, acc_scr, p_scr, s_even, s_odd):
    i = pl.program_id(2)
    nb = ksum_ref.shape[1]
    blk = MOBA_BLOCK
    c = HEAD_DIM ** -0.5 * LOG2_E
    neg = jnp.float32(-jnp.inf)
    heads = range(ATTN_HEADS)
    cols = [slice(g * HEAD_DIM, (g + 1) * HEAD_DIM) for g in heads]
    pos = lax.broadcasted_iota(jnp.int32, (nb, blk), 0).astype(F32)
    n_past = i.astype(F32)

    def scores(j, dst):
        rows = pl.ds(pl.multiple_of(j * blk, blk), blk)
        for g in heads:
            dst[g] = jnp.dot(kb_ref[rows, cols[g]], qt_scr[g], preferred_element_type=F32)

    def pv(j, alphas):
        for g in heads:
            upd = jnp.dot(vt_ref[j, g], p_scr[g], preferred_element_type=F32)
            acc_scr[g] = upd if alphas is None else alphas[g] * acc_scr[g] + upd

    for g in heads:
        qt = q_ref[:, cols[g]].T
        qt_scr[g] = (qt * c).astype(BF16)
        kbar = ksum_ref[0, :, cols[g]] * (1.0 / blk)
        gate = jnp.dot(kbar, qt, precision=lax.Precision.HIGHEST, preferred_element_type=F32)
        sel = jnp.zeros((nb, blk), F32)
        for first in _top3_select(gate, pos, n_past):
            sel = jnp.where((pos == first) & (pos < n_past), 1.0, sel)
        sel_scr[g] = sel

    scores(i, s_odd)
    scores(0, s_even)
    key = lax.broadcasted_iota(jnp.int32, (blk, blk), 0)
    qry = lax.broadcasted_iota(jnp.int32, (blk, blk), 1)
    for g in heads:
        s = jnp.where(key <= qry, s_odd[g], neg)
        m = jnp.max(s, axis=0, keepdims=True)
        p = jnp.exp2(s - m)
        m_scr[g] = m
        p_scr[g] = p.astype(BF16)
    pv(i, None)

    def past_block(j, src, dst):
        if dst is not None:
            scores(j + 1, dst)
        alphas = []
        for g in heads:
            s = src[g]
            chosen = sel_scr[g, pl.ds(j, 1), :] > 0.5
            m_old = m_scr[g]
            m_new = jnp.where(chosen, jnp.maximum(m_old, jnp.max(s, axis=0, keepdims=True)), m_old)
            p = jnp.exp2(s - jnp.where(chosen, m_new, jnp.float32(jnp.inf)))
            alpha = jnp.exp2(m_old - m_new)
            m_scr[g] = m_new
            p_scr[g] = p.astype(BF16)
            alphas.append(alpha)
        pv(j, alphas)

    def two_blocks(t, _):
        past_block(2 * t, s_even, s_odd)
        past_block(2 * t + 1, s_odd, s_even)
        return 0

    lax.fori_loop(0, i // 2, two_blocks, 0)

    @pl.when(i % 2 == 1)
    def _():
        past_block(i - 1, s_even, None)

    for g in heads:
        out = acc_scr[g, 0:HEAD_DIM, :] / acc_scr[g, HEAD_DIM:HEAD_DIM + 1, :]
        o_ref[:, cols[g]] = out.T.astype(o_ref.dtype)


def _prompt_attn(q, kb, vt, ksum, batch, seq):
    n = q.shape[0]
    nb = seq // MOBA_BLOCK
    gw = ATTN_HEADS * HEAD_DIM
    ksum = ksum.reshape(batch, nb, D_MODEL)
    return pl.pallas_call(
        _prompt_attn_kernel,
        grid=(batch, N_HEADS // ATTN_HEADS, nb),
        in_specs=[pl.BlockSpec((MOBA_BLOCK, gw), lambda b, h, i: (b * nb + i, h)),
                  pl.BlockSpec((seq, gw), lambda b, h, i: (b, h)),
                  pl.BlockSpec((nb, ATTN_HEADS, VT_ROWS, MOBA_BLOCK), lambda b, h, i: (b, h, 0, 0)),
                  pl.BlockSpec((1, nb, gw), lambda b, h, i: (b, 0, h))],
        out_specs=pl.BlockSpec((MOBA_BLOCK, gw), lambda b, h, i: (b * nb + i, h)),
        out_shape=jax.ShapeDtypeStruct((n, D_MODEL), BF16),
        scratch_shapes=[pltpu.VMEM((ATTN_HEADS, HEAD_DIM, MOBA_BLOCK), BF16),
                        pltpu.VMEM((ATTN_HEADS, nb, MOBA_BLOCK), F32),
                        pltpu.VMEM((ATTN_HEADS, 1, MOBA_BLOCK), F32),
                        pltpu.VMEM((ATTN_HEADS, VT_ROWS, MOBA_BLOCK), F32),
                        pltpu.VMEM((ATTN_HEADS, MOBA_BLOCK, MOBA_BLOCK), BF16),
                        pltpu.VMEM((ATTN_HEADS, MOBA_BLOCK, MOBA_BLOCK), F32),
                        pltpu.VMEM((ATTN_HEADS, MOBA_BLOCK, MOBA_BLOCK), F32)],
        compiler_params=_params("parallel", "parallel", "arbitrary"),
        name="prompt_attn",
    )(q, kb, vt, ksum)


def _merge_kernel(x_ref, ya_ref, yb_ref, sga_ref, sgb_ref, wba_ref, wbb_ref, wout_ref, o_ref):
    a = jnp.dot(ya_ref[...], wba_ref[...], preferred_element_type=F32)
    b = jnp.dot(yb_ref[...], wbb_ref[...], preferred_element_type=F32)
    mix = (sga_ref[...] * a + sgb_ref[...] * b).astype(BF16)
    o_ref[...] = x_ref[...] + jnp.dot(mix, wout_ref[...], preferred_element_type=F32)


def _merge(x, ya, yb, sga, sgb, wba, wbb, wout, tm):
    n = x.shape[0]
    row_spec = pl.BlockSpec((tm, D_MODEL), lambda i: (i, 0))
    w_spec = _const_spec((D_MODEL, D_MODEL))
    return pl.pallas_call(
        _merge_kernel,
        grid=(n // tm,),
        in_specs=[row_spec] * 5 + [w_spec] * 3,
        out_specs=row_spec,
        out_shape=jax.ShapeDtypeStruct((n, D_MODEL), F32),
        compiler_params=_params("parallel"),
        name="merge",
    )(x, ya, yb, sga, sgb, wba, wbb, wout)


def _prompt_ffn_kernel(pt_ref, x_ref, n2_ref, wu_ref, cw_ref, cb_ref, wd_ref, cache_ref,
                       o_ref, st_ref, bsum_ref, h_scr, carry_scr, page_buf, page_sem,
                       *, tiles_per_seq, layer, first_page, pages_per_step):
    tm = x_ref.shape[0]
    pages = _PageSums(pt_ref, cache_ref, bsum_ref, page_buf, page_sem, layer=layer,
                      first_page=first_page, pages_per_step=pages_per_step)
    pages.prime()

    @pl.when(pl.program_id(0) % tiles_per_seq == 0)
    def _():
        carry_scr[...] = jnp.zeros_like(carry_scr)

    x = x_ref[...]
    h_scr[...] = _rms(x, n2_ref[...]).astype(BF16)
    row = lax.broadcasted_iota(jnp.int32, (8, FF_CHUNK), 0)

    def up_proj(part, c):
        if c >= N_FF_CHUNKS:
            return None
        return jnp.dot(h_scr[...], wu_ref[part, c], preferred_element_type=F32)

    def conv(up, part, c):
        prev = carry_scr[part, c]
        r1, r2 = pltpu.roll(up, 1, 0), pltpu.roll(up, 2, 0)
        top1 = jnp.where(row == 0, prev[1:2, :], r1[0:8, :])
        top2 = jnp.where(row == 0, prev[0:1, :], jnp.where(row == 1, prev[1:2, :], r2[0:8, :]))
        u1 = jnp.concatenate([top1, r1[8:, :]], axis=0)
        u2 = jnp.concatenate([top2, r2[8:, :]], axis=0)
        carry_scr[part, c] = up[tm - 2:tm, :]
        cw = cw_ref[part, c]
        return cb_ref[part, c] + cw[0:1, :] * u2 + cw[1:2, :] * u1 + cw[2:3, :] * up

    acc = x
    up_g, up_z = up_proj(0, 0), up_proj(1, 0)
    for c in range(N_FF_CHUNKS):
        pages.wait_point(c, N_FF_CHUNKS)
        next_g = up_proj(0, c + 1)
        gate = jax.nn.gelu(conv(up_g, 0, c))
        next_z = up_proj(1, c + 1)
        act = (gate * conv(up_z, 1, c)).astype(BF16)
        acc = acc + jnp.dot(act, wd_ref[c], preferred_element_type=F32)
        pages.reduce_point(c, N_FF_CHUNKS)
        pages.after_point(c, N_FF_CHUNKS)
        up_g, up_z = next_g, next_z
    o_ref[...] = acc
    st_ref[0] = carry_scr[...]


def _page_sum_specs(n_steps, pages_per_step):
    blocks = pages_per_step // PAGES_PER_BLOCK
    out_spec = pl.BlockSpec((blocks, N_HEADS, HEAD_DIM), lambda i, pt: (i, 0, 0))
    out_shape = jax.ShapeDtypeStruct((n_steps * blocks, N_HEADS, HEAD_DIM), F32)
    scratch = [pltpu.VMEM((PAGE_RING_SLOTS, PAGE_SUB_BATCH, PAGE_SIZE, N_HEADS, HEAD_DIM), F32),
               pltpu.SemaphoreType.DMA((PAGE_RING_SLOTS,))]
    return out_spec, out_shape, scratch


def _prompt_ffn(x, n2, wu_c, cw_c, cb_c, wd_c, batch, seq, page_table_flat, cache_k, layer, first_page,
                pages_per_step):
    n = x.shape[0]
    tm = ROW_TILE
    tiles_per_seq = seq // tm
    n_steps = n // tm
    row_spec = pl.BlockSpec((tm, D_MODEL), lambda i, pt: (i, 0))
    st_shape = (2, N_FF_CHUNKS, CONV_W - 1, FF_CHUNK)
    bsum_spec, bsum_shape, page_scratch = _page_sum_specs(n_steps, pages_per_step)
    return pl.pallas_call(
        functools.partial(_prompt_ffn_kernel, tiles_per_seq=tiles_per_seq, layer=layer,
                          first_page=first_page, pages_per_step=pages_per_step),
        grid_spec=pltpu.PrefetchScalarGridSpec(
            num_scalar_prefetch=1,
            grid=(n_steps,),
            in_specs=[row_spec, _const_spec((1, D_MODEL)),
                      _const_spec((2, N_FF_CHUNKS, D_MODEL, FF_CHUNK)),
                      _const_spec((2, N_FF_CHUNKS, CONV_W, FF_CHUNK)),
                      _const_spec((2, N_FF_CHUNKS, 1, FF_CHUNK)),
                      _const_spec((N_FF_CHUNKS, FF_CHUNK, D_MODEL)),
                      pl.BlockSpec(memory_space=pl.ANY)],
            out_specs=[row_spec,
                       pl.BlockSpec((1,) + st_shape, lambda i, pt: (i // tiles_per_seq, 0, 0, 0, 0)),
                       bsum_spec],
            scratch_shapes=[pltpu.VMEM((tm, D_MODEL), BF16), pltpu.VMEM(st_shape, F32)] + page_scratch,
        ),
        out_shape=[jax.ShapeDtypeStruct((n, D_MODEL), F32),
                   jax.ShapeDtypeStruct((batch,) + st_shape, F32),
                   bsum_shape],
        compiler_params=_params("arbitrary"),
        name="prompt_ffn",
    )(page_table_flat, x, n2, wu_c, cw_c, cb_c, wd_c, cache_k)


def _sample_in_kernel(x_ref, n1_ref, w_ref, vn_ref, qn_ref, kn_ref, wd0_ref, b0_ref,
                      ya_ref, q_ref, k_ref, vv_ref, gv_ref, sga_ref, sgb_ref):
    h = _rms(x_ref[...], n1_ref[...]).astype(BF16)

    def seg(s):
        return jnp.dot(h, w_ref[:, s * D_MODEL:(s + 1) * D_MODEL], preferred_element_type=F32)

    u = jax.nn.gelu(seg(0))
    v = _rms(jax.nn.gelu(seg(1)), vn_ref[...])
    gv_ref[...] = v
    ya_ref[...] = (u * (wd0_ref[...] * v + b0_ref[...])).astype(BF16)
    aq = seg(2)
    ak = seg(3)
    for hd in range(N_HEADS):
        cs = slice(hd * HEAD_DIM, (hd + 1) * HEAD_DIM)
        q_ref[:, cs] = _rms(aq[:, cs], qn_ref[...])
        k_ref[:, cs] = _rms(ak[:, cs], kn_ref[...])
    vv_ref[...] = seg(4)
    sga_ref[...] = jax.nn.sigmoid(seg(5))
    sgb_ref[...] = jax.nn.sigmoid(seg(6))


def _sample_in(x, n1, w_in, vn, qn, kn, wd0, b0):
    n = x.shape[0]
    f32_out = jax.ShapeDtypeStruct((n, D_MODEL), F32)
    return pl.pallas_call(
        _sample_in_kernel,
        out_shape=[jax.ShapeDtypeStruct((n, D_MODEL), BF16)] + [f32_out] * 6,
        compiler_params=pltpu.CompilerParams(vmem_limit_bytes=VMEM_LIMIT),
        name="sample_in",
    )(x, n1, w_in, vn, qn, kn, wd0, b0)


def _sample_gate_kernel(q_ref, k_ref, bsum_ref, idx_ref):
    n_seq, n_past = bsum_ref.shape[0], bsum_ref.shape[1]
    inv = 1.0 / MOBA_BLOCK
    ones = jnp.ones((HEAD_DIM, HEAD_DIM), F32)
    pos = lax.broadcasted_iota(jnp.int32, (n_past + 1, N_HEADS, HEAD_DIM), 0).astype(F32)
    lane = lax.broadcasted_iota(jnp.int32, (N_HEADS, HEAD_DIM), 1)

    def one_sequence(b, _):
        q = q_ref[b]
        past = (bsum_ref[b] * inv) * q
        own = (k_ref[b] * inv) * q
        prod = jnp.concatenate([past, own[None]], axis=0)
        gate = jnp.dot(prod.reshape((n_past + 1) * N_HEADS, HEAD_DIM), ones,
                       precision=lax.Precision.HIGHEST, preferred_element_type=F32)
        gate = gate.reshape(n_past + 1, N_HEADS, HEAD_DIM)
        out = jnp.zeros((N_HEADS, HEAD_DIM), jnp.int32)
        for r, first in enumerate(_top3_select(gate, pos, jnp.float32(n_past))):
            out = jnp.where(lane == r, first[0].astype(jnp.int32), out)
        idx_ref[b] = out
        return 0

    lax.fori_loop(0, n_seq, one_sequence, 0)


def _sample_gate(q, k, bsum):
    return pl.pallas_call(
        _sample_gate_kernel,
        out_shape=jax.ShapeDtypeStruct(q.shape, jnp.int32),
        compiler_params=pltpu.CompilerParams(vmem_limit_bytes=VMEM_LIMIT),
        name="sample_gate",
    )(q, k, bsum)


def _sample_attn_kernel(pt_ref, idx_ref, q_ref, kn_ref, vn_ref, ck_ref, cv_ref, o_ref,
                        kbuf, vbuf, sem, *, layer, n_pages):
    b = pl.program_id(0)
    n_seq = pl.num_programs(0)

    def page_copies(seq, slot):
        out = []
        for h in range(N_HEADS):
            for j in range(N_SEL_PAGES):
                r, p = divmod(j, PAGES_PER_BLOCK)
                blk = idx_ref[(seq * N_HEADS + h) * MOBA_TOPK + r]
                phys = pt_ref[seq * n_pages + blk * PAGES_PER_BLOCK + p]
                rows = pl.ds(j * PAGE_SIZE, PAGE_SIZE)
                out.append(pltpu.make_async_copy(ck_ref.at[layer, phys, :, h, :],
                                                 kbuf.at[slot, h, rows, :], sem.at[slot]))
                out.append(pltpu.make_async_copy(cv_ref.at[layer, phys, :, h, :],
                                                 vbuf.at[slot, h, rows, :], sem.at[slot]))
        return out

    @pl.when(b == 0)
    def _():
        for cp in page_copies(0, 0):
            cp.start()

    @pl.when(b + 1 < n_seq)
    def _():
        for cp in page_copies(b + 1, (b + 1) % 2):
            cp.start()

    slot = b % 2
    for cp in page_copies(b, slot):
        cp.wait()

    scale = HEAD_DIM ** -0.5
    for h in range(N_HEADS):
        q = q_ref[0, h:h + 1, :]
        s = jnp.sum(kbuf[slot, h] * q, axis=-1, keepdims=True) * scale
        s_own = jnp.sum(kn_ref[0, h:h + 1, :] * q, axis=-1, keepdims=True) * scale
        m = jnp.maximum(jnp.max(s, axis=0, keepdims=True), s_own)
        p = jnp.exp(s - m)
        p_own = jnp.exp(s_own - m)
        l = jnp.sum(p, axis=0, keepdims=True) + p_own
        acc = jnp.sum(p * vbuf[slot, h], axis=0, keepdims=True) + p_own * vn_ref[0, h:h + 1, :]
        o_ref[0, h:h + 1, :] = (acc / l).astype(o_ref.dtype)


def _sample_attn(q, k_new, v_new, cache_k, cache_v, page_table_flat, idx_flat, layer, n_pages):
    n = q.shape[0]
    vec_spec = pl.BlockSpec((1, N_HEADS, HEAD_DIM), lambda b, pt, ix: (b, 0, 0))
    any_spec = pl.BlockSpec(memory_space=pl.ANY)
    buf = pltpu.VMEM((2, N_HEADS, N_SEL_PAGES * PAGE_SIZE, HEAD_DIM), F32)
    return pl.pallas_call(
        functools.partial(_sample_attn_kernel, layer=layer, n_pages=n_pages),
        grid_spec=pltpu.PrefetchScalarGridSpec(
            num_scalar_prefetch=2,
            grid=(n,),
            in_specs=[vec_spec, vec_spec, vec_spec, any_spec, any_spec],
            out_specs=vec_spec,
            scratch_shapes=[buf, buf, pltpu.SemaphoreType.DMA((2,))],
        ),
        out_shape=jax.ShapeDtypeStruct((n, N_HEADS, HEAD_DIM), BF16),
        compiler_params=_params("arbitrary"),
        name="sample_attn",
    )(page_table_flat, idx_flat, q, k_new, v_new, cache_k, cache_v)


def _sample_out_kernel(x_ref, ya_ref, yb_ref, sga_ref, sgb_ref, wba_ref, wbb_ref, wout_ref,
                       n2_ref, wu_ref, cw_ref, cb_ref, wd_ref, h0_ref, h1_ref, o_ref, up_ref):
    a = jnp.dot(ya_ref[...], wba_ref[...], preferred_element_type=F32)
    b = jnp.dot(yb_ref[...], wbb_ref[...], preferred_element_type=F32)
    mix = (sga_ref[...] * a + sgb_ref[...] * b).astype(BF16)
    x = x_ref[...] + jnp.dot(mix, wout_ref[...], preferred_element_type=F32)
    h = _rms(x, n2_ref[...]).astype(BF16)
    up = jnp.dot(h, wu_ref[...], preferred_element_type=F32)
    up_ref[...] = up
    c = cb_ref[...] + cw_ref[0:1, :] * h0_ref[...] + cw_ref[1:2, :] * h1_ref[...] + cw_ref[2:3, :] * up
    act = (jax.nn.gelu(c[:, :D_FF]) * c[:, D_FF:]).astype(BF16)
    o_ref[...] = x + jnp.dot(act, wd_ref[...], preferred_element_type=F32)


def _sample_out(x, ya, yb, sga, sgb, wba, wbb, wout, n2, wu, cw, cb, wd, h0, h1):
    n = x.shape[0]
    return pl.pallas_call(
        _sample_out_kernel,
        out_shape=[jax.ShapeDtypeStruct((n, D_MODEL), F32), jax.ShapeDtypeStruct((n, 2 * D_FF), F32)],
        compiler_params=pltpu.CompilerParams(vmem_limit_bytes=VMEM_LIMIT),
        name="sample_out",
    )(x, ya, yb, sga, sgb, wba, wbb, wout, n2, wu, cw, cb, wd, h0, h1)


def _chunk_cols(a):
    lead = a.shape[:-1]
    a = a.reshape(lead + (2, N_FF_CHUNKS, FF_CHUNK))
    return jnp.moveaxis(a, (-3, -2), (0, 1))


def kernel(x_prompt, x_sample, cache_k, cache_v, state_conv, page_table, norm1_w, w_in, gmlp_vnorm_w, gmlp_ws, gmlp_bs, q_norm_w, k_norm_w, w_branch_a, w_branch_b, w_out, norm2_w, w_up, conv_w, conv_b, w_down):
    batch, seq, _ = x_prompt.shape
    n_seq, dec_seq, _ = x_sample.shape
    depth = w_in.shape[0]
    n_pages = page_table.shape[1]
    assert dec_seq == 1 and seq % ROW_TILE == 0 and ROW_TILE % MOBA_BLOCK == 0
    assert n_pages % PAGES_PER_BLOCK == 0
    n_tiles = batch * seq // ROW_TILE
    assert (n_seq * n_pages) % n_tiles == 0
    pages_per_tile = n_seq * n_pages // n_tiles

    pt_flat = page_table.reshape(-1)
    heads = lambda a: a.reshape(n_seq, N_HEADS, HEAD_DIM)

    xp = x_prompt.reshape(batch * seq, D_MODEL)
    xs = x_sample.reshape(n_seq, D_MODEL)
    row = lambda a: a.reshape(1, -1)
    k_p, v_p, k_s, v_s, gv_s, cv_p, cv_s = [], [], [], [], [], [], []
    for l in range(depth):
        w_in_b = w_in[l].astype(BF16)
        wba, wbb, wout = w_branch_a[l].astype(BF16), w_branch_b[l].astype(BF16), w_out[l].astype(BF16)
        wu_b, wd_b = w_up[l].astype(BF16), w_down[l].astype(BF16)
        n1, n2, vn, qn, kn = row(norm1_w[l]), row(norm2_w[l]), row(gmlp_vnorm_w[l]), row(q_norm_w[l]), row(k_norm_w[l])

        ya, q, k, vv, kb, vt, ksum, sga, sgb = _prompt_in(xp, n1, w_in_b, vn, qn, kn, gmlp_ws[l], gmlp_bs[l].T)
        yb = _prompt_attn(q, kb, vt, ksum, batch, seq)
        xp = _merge(xp, ya, yb, sga, sgb, wba, wbb, wout, MERGE_ROW_TILE)
        xp, st, bsum = _prompt_ffn(xp, n2, _chunk_cols(wu_b), _chunk_cols(conv_w[l]),
                                   _chunk_cols(conv_b[l].reshape(1, -1)),
                                   wd_b.reshape(N_FF_CHUNKS, FF_CHUNK, D_MODEL), batch, seq,
                                   pt_flat, cache_k, l, 0, pages_per_tile)
        bsum = bsum.reshape(n_seq, n_pages // PAGES_PER_BLOCK, N_HEADS, HEAD_DIM)
        k_p.append(k.reshape(batch, seq, N_HEADS, HEAD_DIM))
        v_p.append(vv.reshape(batch, seq, N_HEADS, HEAD_DIM))
        cv_p.append(jnp.transpose(st, (0, 3, 1, 2, 4)).reshape(batch, CONV_W - 1, 2 * D_FF))

        wd0 = jnp.repeat(gmlp_ws[l][:, 0, 0], GROUP_DIM).reshape(1, -1)
        b0 = jnp.repeat(gmlp_bs[l][:, 0], GROUP_DIM).reshape(1, -1)
        ya, q, k, vv, gv, sga, sgb = _sample_in(xs, n1, w_in_b, vn, qn, kn, wd0, b0)
        idx = _sample_gate(heads(q), heads(k), bsum)
        idx_flat = idx[:, :, :MOBA_TOPK].reshape(-1)
        yb = _sample_attn(heads(q), heads(k), heads(vv), cache_k, cache_v, pt_flat, idx_flat, l, n_pages)
        yb = yb.reshape(n_seq, D_MODEL)
        hist = state_conv[l]
        xs, up = _sample_out(xs, ya, yb, sga, sgb, wba, wbb, wout, n2, wu_b, conv_w[l],
                             conv_b[l].reshape(1, -1), wd_b, hist[:, 0, :], hist[:, 1, :])
        k_s.append(k.reshape(n_seq, 1, N_HEADS, HEAD_DIM))
        v_s.append(vv.reshape(n_seq, 1, N_HEADS, HEAD_DIM))
        gv_s.append(gv.reshape(n_seq, 1, D_MODEL))
        cv_s.append(jnp.stack([hist[:, 1, :], up], axis=1))

    return (xp.reshape(batch, seq, D_MODEL), xs.reshape(n_seq, 1, D_MODEL),
            jnp.stack(k_p), jnp.stack(v_p), jnp.stack(k_s), jnp.stack(v_s),
            jnp.stack(gv_s), jnp.stack(cv_p), jnp.stack(cv_s))
```
